```python
import jax, jax.numpy as jnp
from jax import lax
import numpy as np

D_MODEL = 2048
BATCH = 1
SEQ = 8192
DEPTH = 2
DEC_BATCH = 32
DEC_SEQ = 8
PAST_LEN = 8192
PAGE_SIZE = 128

HEAD_DIM = 64
D_A = D_MODEL // 2
H_A = D_A // HEAD_DIM
D_B = D_MODEL // 2
H_B = D_B // HEAD_DIM
D_FF = 3 * D_MODEL
CONV_W = 3
LORA_W = 96
LORA_A = 96
LORA_V = 64
LORA_G = 256
Q_BLOCK = 128
RMS_EPS = 1e-6
GN_EPS = 64e-5
D_IN = 3 * D_A + 3 * D_B + 2 * D_MODEL

kernel_name = 'rwkv7_stickbreak_hybrid_step'


def rms_norm(x, g):
    xf = x.astype(jnp.float32)
    y = xf * lax.rsqrt(jnp.mean(xf * xf, axis=-1, keepdims=True) + RMS_EPS)
    return (y * g.astype(jnp.float32)).astype(x.dtype)


def shift_rows(x, first):
    return jnp.concatenate([first[:, None].astype(x.dtype), x[:, :-1]], axis=1)


def heads(t):
    return t.reshape(*t.shape[:-1], t.shape[-1] // HEAD_DIM, HEAD_DIM).astype(jnp.float32)


def wkv7_scan(r, w, k, v, a, b, s0):
    def step(s, inp):
        r_t, w_t, k_t, v_t, a_t, b_t = inp
        sa = jnp.einsum('bhvk,bhk->bhv', s, a_t)
        s = s * w_t[:, :, None, :] + sa[..., None] * b_t[:, :, None, :] + v_t[..., None] * k_t[:, :, None, :]
        return s, jnp.einsum('bhvk,bhk->bhv', s, r_t)
    xs = tuple(jnp.moveaxis(t, 1, 0) for t in (r, w, k, v, a, b))
    s_fin, ys = lax.scan(step, s0, xs)
    return jnp.moveaxis(ys, 0, 1), s_fin


def stick_break(q, keys, vals, q_pos, k_pos, bias):
    z = jnp.einsum('bqhd,bshd->bhqs', q.astype(jnp.float32), keys.astype(jnp.float32)) * (HEAD_DIM ** -0.5)
    z = z + bias.astype(jnp.float32)[None, :, None, None]
    mask = k_pos[None, :] < q_pos[:, None]
    sp = jnp.where(mask, jax.nn.softplus(z), 0.0)
    c = lax.cumsum(sp, axis=3, reverse=True)
    att = jnp.exp(jnp.where(mask, z - c, -jnp.inf))
    return jnp.einsum('bhqs,bshd->bqhd', att, vals.astype(jnp.float32)).astype(q.dtype)


def sb_prompt(q, k, v, bias):
    B, T, H, D = q.shape
    nb = T // Q_BLOCK
    qb = jnp.moveaxis(q.reshape(B, nb, Q_BLOCK, H, D), 1, 0)
    k_pos = jnp.arange(T)
    def blk(args):
        q_i, i = args
        return stick_break(q_i, k, v, i * Q_BLOCK + jnp.arange(Q_BLOCK), k_pos, bias)
    out = lax.map(blk, (qb, jnp.arange(nb)))
    return jnp.moveaxis(out, 0, 1).reshape(B, T, H, D)


def sb_sample(q, k_new, v_new, pool_k, pool_v, page_table, bias):
    Bd, T = q.shape[:2]
    n_pages = page_table.shape[1]
    past = n_pages * PAGE_SIZE
    past_k = jnp.take(pool_k, page_table, axis=0).reshape(Bd, past, H_B, HEAD_DIM)
    past_v = jnp.take(pool_v, page_table, axis=0).reshape(Bd, past, H_B, HEAD_DIM)
    keys = jnp.concatenate([past_k, k_new.astype(past_k.dtype)], axis=1)
    vals = jnp.concatenate([past_v, v_new.astype(past_v.dtype)], axis=1)
    return stick_break(q, keys, vals, past + jnp.arange(T), jnp.arange(past + T), bias)


def setup_inputs(seed: int = 0) -> dict:
    key = jax.random.key(seed)
    ks = iter(jax.random.split(key, 64))
    def nrm(shape, scale):
        return jax.random.normal(next(ks), shape, jnp.float32) * scale
    def uni(shape, lo, hi):
        return jax.random.uniform(next(ks), shape, jnp.float32, lo, hi)
    n_pages = PAST_LEN // PAGE_SIZE
    n_pool = (DEC_BATCH * n_pages * 5) // 4
    page_table = jax.random.permutation(next(ks), n_pool)[:DEC_BATCH * n_pages].reshape(DEC_BATCH, n_pages).astype(jnp.int32)
    dv = DEPTH - 1
    D = D_MODEL
    return {
        'x_prompt': nrm((BATCH, SEQ, D), 1.0),
        'x_sample': nrm((DEC_BATCH, DEC_SEQ, D), 1.0),
        'cache_sb_k': nrm((DEPTH, n_pool, PAGE_SIZE, H_B, HEAD_DIM), 1.0),
        'cache_sb_v': nrm((DEPTH, n_pool, PAGE_SIZE, H_B, HEAD_DIM), 1.0),
        'state_shift': nrm((DEPTH, DEC_BATCH, D), 1.0),
        'state_wkv': nrm((DEPTH, DEC_BATCH, H_A, HEAD_DIM, HEAD_DIM), 0.3),
        'state_conv': nrm((DEPTH, DEC_BATCH, CONV_W - 1, D_FF), 1.0),
        'page_table': page_table,
        'norm_mix': 1.0 + nrm((DEPTH, D), 0.02),
        'w_in': nrm((DEPTH, D, D_IN), D ** -0.5),
        'mu_rkv': uni((DEPTH, 3 * D_A), 0.0, 1.0),
        'mu_x': uni((DEPTH, 3, D), 0.0, 1.0),
        'w0': uni((DEPTH, D_A), -2.0, 0.0),
        'w1': nrm((DEPTH, D, LORA_W), D ** -0.5),
        'w2': nrm((DEPTH, LORA_W, D_A), 0.1 * LORA_W ** -0.5),
        'a0': nrm((DEPTH, D_A), 0.1),
        'a1': nrm((DEPTH, D, LORA_A), D ** -0.5),
        'a2': nrm((DEPTH, LORA_A, D_A), 0.1 * LORA_A ** -0.5),
        'g1': nrm((DEPTH, D, LORA_G), D ** -0.5),
        'g2': nrm((DEPTH, LORA_G, D_A), LORA_G ** -0.5),
        'mu_vr': uni((dv, D), 0.0, 1.0),
        'v0': nrm((dv, D_A), 0.1),
        'v1': nrm((dv, D, LORA_V), D ** -0.5),
        'v2': nrm((dv, LORA_V, D_A), 0.1 * LORA_V ** -0.5),
        'k_k': 0.85 + nrm((DEPTH, D_A), 0.05),
        'k_a': 1.0 + nrm((DEPTH, D_A), 0.05),
        'r_k': nrm((DEPTH, H_A, HEAD_DIM), 0.1),
        'ln_x_w': 1.0 + nrm((DEPTH, D_A), 0.02),
        'ln_x_b': nrm((DEPTH, D_A), 0.02),
        'w_a': nrm((DEPTH, D_A, D), D_A ** -0.5),
        'sb_bias': uni((DEPTH, H_B), -9.0, -3.0),
        'w_b': nrm((DEPTH, D_B, D), D_B ** -0.5),
        'w_o': nrm((DEPTH, D, D), 0.5 * D ** -0.5),
        'norm_ffn': 1.0 + nrm((DEPTH, D), 0.02),
        'w_up': nrm((DEPTH, D, D_FF), D ** -0.5),
        'w_gate': nrm((DEPTH, D, D_FF), D ** -0.5),
        'conv_w': nrm((DEPTH, CONV_W, D_FF), CONV_W ** -0.5),
        'conv_b': nrm((DEPTH, D_FF), 0.02),
        'w_down': nrm((DEPTH, D_FF, D), 0.5 * D_FF ** -0.5),
        'norm_final': 1.0 + nrm((D,), 0.02),
    }


def reference(x_prompt, x_sample, cache_sb_k, cache_sb_v, state_shift, state_wkv, state_conv, page_table,
              norm_mix, w_in, mu_rkv, mu_x, w0, w1, w2, a0, a1, a2, g1, g2, mu_vr, v0, v1, v2,
              k_k, k_a, r_k, ln_x_w, ln_x_b, w_a, sb_bias, w_b, w_o, norm_ffn, w_up, w_gate, conv_w, conv_b,
              w_down, norm_final):

    def layer(x, l, shift_prev, wkv_prev, conv_prev, v_first, attend):
        B, T, _ = x.shape
        xn = rms_norm(x, norm_mix[l])
        proj = xn @ w_in[l]
        p_rkv = proj[..., :3 * D_A]
        p_sb = proj[..., 3 * D_A:3 * D_A + 3 * D_B]
        p_gate = proj[..., 3 * D_A + 3 * D_B:]
        prev_rkv = shift_prev.astype(x.dtype) @ w_in[l][:, :3 * D_A]
        rkv = p_rkv + (shift_rows(p_rkv, prev_rkv) - p_rkv) * mu_rkv[l]
        r, k, v = jnp.split(rkv, 3, axis=-1)
        xx = shift_rows(xn, shift_prev) - xn
        xw = xn + xx * mu_x[l, 0]
        xa = xn + xx * mu_x[l, 1]
        xg = xn + xx * mu_x[l, 2]
        w_log = -jax.nn.softplus(-(w0[l] + jnp.tanh(xw @ w1[l]) @ w2[l])) - 0.5
        decay = jnp.exp(-jnp.exp(w_log.astype(jnp.float32)))
        a_rate = jax.nn.sigmoid(a0[l] + (xa @ a1[l]) @ a2[l])
        g_out = jax.nn.sigmoid(xg @ g1[l]) @ g2[l]
        if l == 0:
            v_first = v
        else:
            xv = xn + xx * mu_vr[l - 1]
            v = v + (v_first - v) * jax.nn.sigmoid(v0[l - 1] + (xv @ v1[l - 1]) @ v2[l - 1])
        r_h, v_h, a_h, d_h = heads(r), heads(v), heads(a_rate), heads(decay)
        kk = heads(k * k_k[l])
        kk = kk * lax.rsqrt(jnp.sum(kk * kk, axis=-1, keepdims=True) + 1e-12)
        k_h = heads(k) * (1.0 + (a_h - 1.0) * heads(k_a[l]))
        y_h, s_new = wkv7_scan(r_h, d_h, k_h, v_h, -kk, kk * a_h, wkv_prev.astype(jnp.float32))
        mu = jnp.mean(y_h, axis=-1, keepdims=True)
        var = jnp.mean(jnp.square(y_h - mu), axis=-1, keepdims=True)
        y_n = ((y_h - mu) * lax.rsqrt(var + GN_EPS)).reshape(B, T, D_A) * ln_x_w[l] + ln_x_b[l]
        bonus = (jnp.sum(r_h * k_h * r_k[l], axis=-1, keepdims=True) * v_h).reshape(B, T, D_A)
        branch_a = ((y_n + bonus).astype(x.dtype) * g_out) @ w_a[l]
        q_s, k_s, v_s = (t.reshape(B, T, H_B, HEAD_DIM) for t in jnp.split(p_sb, 3, axis=-1))
        branch_b = attend(q_s, k_s, v_s, l, sb_bias[l]).reshape(B, T, D_B) @ w_b[l]
        g_a, g_b = jnp.split(p_gate, 2, axis=-1)
        x = x + (jax.nn.sigmoid(g_a) * branch_a + jax.nn.sigmoid(g_b) * branch_b) @ w_o[l]
        xf = rms_norm(x, norm_ffn[l])
        u = xf @ w_up[l]
        gate_seq = jnp.concatenate([conv_prev.astype(x.dtype), xf @ w_gate[l]], axis=1)
        conv = sum((conv_w[l, i] * gate_seq[:, i:i + T] for i in range(CONV_W)), conv_b[l])
        x = x + (jax.nn.silu(conv) * u) @ w_down[l]
        new_state = (xn[:, -1], s_new.astype(x.dtype), gate_seq[:, T:], k_s, v_s)
        return x, v_first, new_state

    def run(x, shift0, wkv0, conv0, attend):
        v_first = None
        outs = []
        for l in range(DEPTH):
            x, v_first, st = layer(x, l, shift0[l], wkv0[l], conv0[l], v_first, attend)
            outs.append(st)
        stacked = [jnp.stack([o[i] for o in outs]) for i in range(5)]
        return rms_norm(x, norm_final), stacked

    Bp = x_prompt.shape[0]
    shift0_p = jnp.zeros((DEPTH, Bp, D_MODEL), x_prompt.dtype)
    wkv0_p = jnp.zeros((DEPTH, Bp, H_A, HEAD_DIM, HEAD_DIM), jnp.float32)
    conv0_p = jnp.zeros((DEPTH, Bp, CONV_W - 1, D_FF), x_prompt.dtype)
    y_prompt, (k_p, v_p, shift_p, wkv_p, conv_p) = run(
        x_prompt, shift0_p, wkv0_p, conv0_p, lambda q, k, v, l, b: sb_prompt(q, k, v, b))
    y_sample, (k_s, v_s, shift_s, wkv_s, conv_s) = run(
        x_sample, state_shift, state_wkv, state_conv,
        lambda q, k, v, l, b: sb_sample(q, k, v, cache_sb_k[l], cache_sb_v[l], page_table, b))
    return (y_prompt, y_sample, k_p, v_p, shift_p, wkv_p, conv_p, k_s, v_s, shift_s, wkv_s, conv_s)
```

```python
import functools

import jax
import jax.numpy as jnp
from jax import lax
from jax.experimental import pallas as pl
from jax.experimental.pallas import tpu as pltpu

F32 = jnp.float32
BF16 = jnp.bfloat16

HEAD_DIM = 64
LANES = 128
PAGE_SIZE = 128
RMS_EPS = 1e-6
GN_EPS = 64e-5
WKV_CHUNK = 64
VMEM_LIMIT = 48 * 1024 * 1024

NN = (((1,), (0,)), ((), ()))
NT = (((1,), (1,)), ((), ()))
TN = (((0,), (0,)), ((), ()))


def _dot(a, b, dims=NN):
    return lax.dot_general(a, b, dims, preferred_element_type=F32)


def _split2(x):
    hi = x.astype(BF16)
    lo = (x - hi.astype(F32)).astype(BF16)
    return hi, lo


def _dot3(a, b, dims=NN):
    ah, al = _split2(a)
    bh, bl = _split2(b)
    return _dot(ah, bh, dims) + (_dot(ah, bl, dims) + _dot(al, bh, dims))


def _pick(dim, candidates):
    for c in candidates:
        if dim % c == 0:
            return c
    return dim


def _mm_kernel_single(x_ref, w_ref, o_ref):
    o_ref[...] = _dot(x_ref[...], w_ref[...]).astype(o_ref.dtype)


def _mm_kernel_acc(x_ref, w_ref, o_ref, acc_ref, *, nk):
    k = pl.program_id(2)

    @pl.when(k == 0)
    def _():
        acc_ref[...] = jnp.zeros_like(acc_ref)

    acc_ref[...] += _dot(x_ref[...], w_ref[...])

    @pl.when(k == nk - 1)
    def _():
        o_ref[...] = acc_ref[...].astype(o_ref.dtype)


def matmul(x, w, out_dtype=F32):
    m, kd = x.shape
    n = w.shape[1]
    tm = _pick(m, (768, 512, 256, 128, 64, 32, 16, 8))
    tn = _pick(n, (1024, 512, 256, 128))
    tk = _pick(kd, (2048, 1024, 512, 256, 128))
    nk = kd // tk
    grid = (m // tm, n // tn, nk)
    params = pltpu.CompilerParams(
        dimension_semantics=("parallel", "parallel", "arbitrary"), vmem_limit_bytes=VMEM_LIMIT)
    in_specs = [pl.BlockSpec((tm, tk), lambda i, j, k: (i, k)),
                pl.BlockSpec((tk, tn), lambda i, j, k: (k, j))]
    out_spec = pl.BlockSpec((tm, tn), lambda i, j, k: (i, j))
    if nk == 1:
        return pl.pallas_call(
            _mm_kernel_single, grid=grid, in_specs=in_specs, out_specs=out_spec,
            out_shape=jax.ShapeDtypeStruct((m, n), out_dtype), compiler_params=params,
            name="mm")(x, w)
    return pl.pallas_call(
        functools.partial(_mm_kernel_acc, nk=nk), grid=grid, in_specs=in_specs, out_specs=out_spec,
        out_shape=jax.ShapeDtypeStruct((m, n), out_dtype),
        scratch_shapes=[pltpu.VMEM((tm, tn), F32)], compiler_params=params, name="mm_acc")(x, w)


def _wkv_pair_terms(r, lw, k, v, a, b, c_len):
    row = lax.broadcasted_iota(jnp.int32, (c_len, c_len), 0)
    col = lax.broadcasted_iota(jnp.int32, (c_len, c_len), 1)
    incl = col <= row
    strict = col < row
    eye = (col == row).astype(F32)
    tri = incl.astype(BF16)
    l1 = lw.astype(BF16)
    rem = lw - l1.astype(F32)
    l2 = rem.astype(BF16)
    l3 = (rem - l2.astype(F32)).astype(BF16)
    lam = _dot(tri, l1) + (_dot(tri, l2) + _dot(tri, l3))
    lam_c = lam[c_len - 1:c_len, :]
    e_neg = jnp.exp(-lam)
    e_end = jnp.exp(lam_c - lam)
    at = a * jnp.exp(lam - lw)
    rt = r * jnp.exp(lam)
    bt = b * e_neg
    kt = k * e_neg
    bh = b * e_end
    kh = k * e_end
    w_c = jnp.exp(lam_c)

    lane = lax.broadcasted_iota(jnp.int32, (c_len, LANES), 1)
    m0 = lane < HEAD_DIM
    n_double = max(c_len.bit_length() - 2, 0)

    per_head = []
    for h in range(2):
        mh = m0 if h == 0 else jnp.logical_not(m0)
        at_h = jnp.where(mh, at, 0.0)
        rt_h = jnp.where(mh, rt, 0.0)
        lab = jnp.where(strict, _dot3(at_h, bt, NT), 0.0)
        lak = jnp.where(strict, _dot3(at_h, kt, NT), 0.0)
        lrb = jnp.where(incl, _dot3(rt_h, bt, NT), 0.0)
        lrk = jnp.where(incl, _dot3(rt_h, kt, NT), 0.0)
        t_inv = eye + lab
        pw = lab
        for _ in range(n_double):
            pw = _dot3(pw, pw)
            t_inv = t_inv + _dot3(t_inv, pw)
        ta_h = _dot3(t_inv, at)
        u0_h = _dot3(t_inv, _dot3(lak, v))
        per_head.append((lrb, lrk, ta_h, u0_h))

    ta = jnp.where(m0, per_head[0][2], per_head[1][2])
    u0 = jnp.where(m0, per_head[0][3], per_head[1][3])
    rc_h = [_dot3(ph[0], ta) for ph in per_head]
    y0_h = [_dot3(ph[0], u0) + _dot3(ph[1], v) for ph in per_head]
    rc = rt + jnp.where(m0, rc_h[0], rc_h[1])
    y0 = jnp.where(m0, y0_h[0], y0_h[1])

    rr = lax.broadcasted_iota(jnp.int32, (LANES, LANES), 0)
    cc = lax.broadcasted_iota(jnp.int32, (LANES, LANES), 1)
    same_head = (rr < HEAD_DIM) == (cc < HEAD_DIM)
    pc = jnp.where(same_head, _dot3(bh, ta, TN), 0.0)
    pc = pc + jnp.where(rr == cc, jnp.broadcast_to(w_c, (LANES, LANES)), 0.0)
    d0 = jnp.where(same_head, _dot3(bh, u0, TN) + _dot3(kh, v, TN), 0.0)
    return rc, y0, pc, d0


def _wkv_terms_kernel(r_ref, lw_ref, k_ref, v_ref, a_ref, b_ref, rc_ref, y0_ref, pc_ref, d0_ref,
                      *, c_len, pairs):
    for p in range(pairs):
        sl = slice(p * LANES, (p + 1) * LANES)
        rc, y0, pc, d0 = _wkv_pair_terms(
            r_ref[:, sl], lw_ref[:, sl], k_ref[:, sl], v_ref[:, sl], a_ref[:, sl], b_ref[:, sl], c_len)
        rc_ref[:, sl] = rc
        y0_ref[:, sl] = y0
        pc_ref[0, p] = pc
        d0_ref[0, p] = d0


def _wkv_sweep_kernel(rc_ref, y0_ref, pc_ref, d0_ref, s0_ref, y_ref, sout_ref, st_ref, *, n_chunks):
    c = pl.program_id(2)

    @pl.when(c == 0)
    def _():
        st_ref[...] = s0_ref[0, 0]

    st = st_ref[...]
    y_ref[...] = _dot3(rc_ref[...], st) + y0_ref[...]
    s_new = _dot3(pc_ref[0, 0], st) + d0_ref[0, 0]
    st_ref[...] = s_new

    @pl.when(c == n_chunks - 1)
    def _():
        sout_ref[0, 0] = s_new


def wkv7(r, lw, k, v, a, b, s0, *, row0, n_seq, n_chunks, c_len):
    width = r.shape[1]
    n_pairs = width // LANES
    pairs = 2 if n_pairs % 2 == 0 else 1
    tot_chunks = n_seq * n_chunks
    rows = tot_chunks * c_len
    blk0 = row0 // c_len
    lw_cols = pairs * LANES
    in_spec = pl.BlockSpec((c_len, lw_cols), lambda c, g: (blk0 + c, g))
    row_spec = pl.BlockSpec((c_len, lw_cols), lambda c, g: (c, g))
    mat_spec = pl.BlockSpec((1, pairs, LANES, LANES), lambda c, g: (c, g, 0, 0))
    rc, y0, pc, d0 = pl.pallas_call(
        functools.partial(_wkv_terms_kernel, c_len=c_len, pairs=pairs),
        grid=(tot_chunks, n_pairs // pairs),
        in_specs=[in_spec] * 6,
        out_specs=[row_spec, row_spec, mat_spec, mat_spec],
        out_shape=[jax.ShapeDtypeStruct((rows, width), F32), jax.ShapeDtypeStruct((rows, width), F32),
                   jax.ShapeDtypeStruct((tot_chunks, n_pairs, LANES, LANES), F32),
                   jax.ShapeDtypeStruct((tot_chunks, n_pairs, LANES, LANES), F32)],
        compiler_params=pltpu.CompilerParams(
            dimension_semantics=("parallel", "parallel"), vmem_limit_bytes=VMEM_LIMIT),
        name="wkv_terms")(r, lw, k, v, a, b)

    rowp = pl.BlockSpec((c_len, LANES), lambda s, p, c: (s * n_chunks + c, p))
    matp = pl.BlockSpec((1, 1, LANES, LANES), lambda s, p, c: (s * n_chunks + c, p, 0, 0))
    stp = pl.BlockSpec((1, 1, LANES, LANES), lambda s, p, c: (s, p, 0, 0))
    y, s_fin = pl.pallas_call(
        functools.partial(_wkv_sweep_kernel, n_chunks=n_chunks),
        grid=(n_seq, n_pairs, n_chunks),
        in_specs=[rowp, rowp, matp, matp, stp],
        out_specs=[rowp, stp],
        out_shape=[jax.ShapeDtypeStruct((rows, width), F32),
                   jax.ShapeDtypeStruct((n_seq, n_pairs, LANES, LANES), F32)],
        scratch_shapes=[pltpu.VMEM((LANES, LANES), F32)],
        compiler_params=pltpu.CompilerParams(
            dimension_semantics=("parallel", "parallel", "arbitrary"), vmem_limit_bytes=VMEM_LIMIT),
        name="wkv_sweep")(rc, y0, pc, d0, s0)
    return y, s_fin


def _softplus(z):
    return jnp.maximum(z, 0.0) + jnp.log1p(jnp.exp(-jnp.abs(z)))


def _suffix_sums(sp, triu):
    hi, lo = _split2(sp)
    return _dot(hi, triu) + _dot(lo, triu)


def _sb_block(z, valid, carry, vb, triu):
    sp = _softplus(z)
    if valid is not None:
        sp = jnp.where(valid, sp, 0.0)
    rc = _suffix_sums(sp, triu)
    att = jnp.exp(z - (rc + carry))
    if valid is not None:
        att = jnp.where(valid, att, 0.0)
    contrib = _dot(att.astype(BF16), vb)
    carry = carry + jnp.broadcast_to(rc[:, 0:1], carry.shape)
    return contrib, carry


def _sb_prompt_kernel(q_ref, bias_ref, k_ref, v_ref, o_ref, *, tq, tk):
    i = pl.program_id(1)
    q = q_ref[...]
    lane = lax.broadcasted_iota(jnp.int32, (tq, LANES), 1)
    m0 = lane < HEAD_DIM
    zero = jnp.zeros_like(q)
    qs = jnp.concatenate([jnp.where(m0, q, zero), jnp.where(m0, zero, q)], axis=0)
    bias = bias_ref[0]
    jr = lax.broadcasted_iota(jnp.int32, (tk, tk), 0)
    jc = lax.broadcasted_iota(jnp.int32, (tk, tk), 1)
    triu = (jr >= jc).astype(BF16)
    row = lax.broadcasted_iota(jnp.int32, (2 * tq, tk), 0)
    col = lax.broadcasted_iota(jnp.int32, (2 * tq, tk), 1)
    q_pos = i * tq + jnp.where(row >= tq, row - tq, row)
    n_diag = tq // tk

    def block(j, acc, carry, masked):
        start = pl.multiple_of(j * tk, tk)
        kb = k_ref[pl.ds(start, tk), :]
        vb = v_ref[pl.ds(start, tk), :]
        z = _dot(qs, kb, NT) + bias
        valid = (j * tk + col) < q_pos if masked else None
        contrib, carry = _sb_block(z, valid, carry, vb, triu)
        return acc + contrib, carry

    acc = jnp.zeros((2 * tq, LANES), F32)
    carry = jnp.zeros((2 * tq, tk), F32)
    j_top = (i + 1) * n_diag - 1
    for d in range(n_diag):
        acc, carry = block(j_top - d, acc, carry, True)

    def body(jj, ac):
        j = i * n_diag - 1 - jj
        return block(j, ac[0], ac[1], False)

    acc, carry = lax.fori_loop(0, i * n_diag, body, (acc, carry))
    o_ref[...] = jnp.where(m0, acc[:tq], acc[tq:]).astype(o_ref.dtype)


def sb_prompt(q, k, v, bias, *, seq_len, tq=128, tk=128):
    width = q.shape[1]
    n_pairs = width // LANES
    nq = seq_len // tq
    bias_cols = jnp.repeat(bias.astype(F32).reshape(n_pairs, 2), tq, axis=1).reshape(n_pairs, 2 * tq, 1)
    return pl.pallas_call(
        functools.partial(_sb_prompt_kernel, tq=tq, tk=tk),
        grid=(n_pairs, nq),
        in_specs=[pl.BlockSpec((tq, LANES), lambda p, i: (i, p)),
                  pl.BlockSpec((1, 2 * tq, 1), lambda p, i: (p, 0, 0)),
                  pl.BlockSpec((seq_len, LANES), lambda p, i: (0, p)),
                  pl.BlockSpec((seq_len, LANES), lambda p, i: (0, p))],
        out_specs=pl.BlockSpec((tq, LANES), lambda p, i: (i, p)),
        out_shape=jax.ShapeDtypeStruct((seq_len, width), BF16),
        compiler_params=pltpu.CompilerParams(
            dimension_semantics=("parallel", "arbitrary"), vmem_limit_bytes=VMEM_LIMIT),
        name="sb_prompt")(q, bias_cols, k, v)


def _sb_sample_kernel(pt_ref, qbd_ref, bias_ref, kn_ref, vn_ref, *rest, n_steps, dec_seq, n_heads, n_pg):
    kp_refs, vp_refs = rest[:n_pg], rest[n_pg:2 * n_pg]
    o_ref, acc_ref, carry_ref = rest[2 * n_pg:]
    j = pl.program_id(1)
    rows = qbd_ref.shape[1]
    jr = lax.broadcasted_iota(jnp.int32, (PAGE_SIZE, PAGE_SIZE), 0)
    jc = lax.broadcasted_iota(jnp.int32, (PAGE_SIZE, PAGE_SIZE), 1)
    triu = (jr >= jc).astype(BF16)
    qbd = qbd_ref[0]
    bias = bias_ref[...]

    @pl.when(j == 0)
    def _():
        row = lax.broadcasted_iota(jnp.int32, (rows, PAGE_SIZE), 0)
        col = lax.broadcasted_iota(jnp.int32, (rows, PAGE_SIZE), 1)
        z = _dot(qbd, kn_ref[0], NT) + bias
        contrib, carry = _sb_block(z, col < row % dec_seq, jnp.zeros((rows, PAGE_SIZE), F32), vn_ref[0], triu)
        acc_ref[...] = contrib
        carry_ref[...] = carry

    @pl.when(j > 0)
    def _():
        carry = carry_ref[...]
        atts = [None] * n_pg
        for i in reversed(range(n_pg)):
            z = _dot(qbd, kp_refs[i][0].astype(BF16)) + bias
            rc = _suffix_sums(_softplus(z), triu)
            atts[i] = jnp.exp(z - (rc + carry)).astype(BF16)
            carry = carry + jnp.broadcast_to(rc[:, 0:1], carry.shape)
        att = jnp.concatenate(atts, axis=1)
        vt = jnp.concatenate([vp_refs[i][0].astype(BF16) for i in range(n_pg)], axis=1)
        acc_ref[...] += _dot(att, vt, NT)
        carry_ref[...] = carry

    @pl.when(j == n_steps - 1)
    def _():
        lane = lax.broadcasted_iota(jnp.int32, (dec_seq, n_heads * HEAD_DIM), 1)
        out = jnp.zeros((dec_seq, n_heads * HEAD_DIM), F32)
        for h in range(n_heads):
            blk = acc_ref[h * dec_seq:(h + 1) * dec_seq, :]
            out = jnp.where(lane // HEAD_DIM == h, blk, out)
        o_ref[0] = out.astype(o_ref.dtype)


def sb_sample(qbd, bias_col, k_new, v_new, pool_kt, pool_vt, page_ids, *, dec_seq, n_heads, n_pg=4):
    bsz, rows, width = qbd.shape
    n_pages = page_ids.shape[0] // bsz
    assert n_pages % n_pg == 0
    n_steps = n_pages // n_pg + 1

    def pool_idx(i):
        return lambda b, j, pt: (pt[b * n_pages + n_pages - n_pg * jnp.maximum(j, 1) + i], 0, 0)

    page_specs = [pl.BlockSpec((1, width, PAGE_SIZE), pool_idx(i)) for i in range(n_pg)]
    grid_spec = pltpu.PrefetchScalarGridSpec(
        num_scalar_prefetch=1, grid=(bsz, n_steps),
        in_specs=[pl.BlockSpec((1, rows, width), lambda b, j, pt: (b, 0, 0)),
                  pl.BlockSpec((rows, 1), lambda b, j, pt: (0, 0)),
                  pl.BlockSpec((1, PAGE_SIZE, width), lambda b, j, pt: (b, 0, 0)),
                  pl.BlockSpec((1, PAGE_SIZE, width), lambda b, j, pt: (b, 0, 0))] + page_specs + page_specs,
        out_specs=pl.BlockSpec((1, dec_seq, width), lambda b, j, pt: (b, 0, 0)),
        scratch_shapes=[pltpu.VMEM((rows, width), F32), pltpu.VMEM((rows, PAGE_SIZE), F32)])
    return pl.pallas_call(
        functools.partial(_sb_sample_kernel, n_steps=n_steps, dec_seq=dec_seq, n_heads=n_heads, n_pg=n_pg),
        grid_spec=grid_spec,
        out_shape=jax.ShapeDtypeStruct((bsz, dec_seq, width), BF16),
        compiler_params=pltpu.CompilerParams(
            dimension_semantics=("parallel", "arbitrary"), vmem_limit_bytes=VMEM_LIMIT),
        name="sb_sample")(page_ids, qbd, bias_col, k_new, v_new, *([pool_kt] * n_pg), *([pool_vt] * n_pg))


def _rms_norm(x, g):
    return x * lax.rsqrt(jnp.mean(x * x, axis=-1, keepdims=True) + RMS_EPS) * g


def _shift_tokens(x, n_prompt, dec_seq, first_sample):
    xp = x[:n_prompt]
    xs = x[n_prompt:].reshape(first_sample.shape[0], dec_seq, -1)
    prev_p = jnp.concatenate([jnp.zeros_like(xp[:1]), xp[:-1]], axis=0)
    prev_s = jnp.concatenate([first_sample[:, None], xs[:, :-1]], axis=1)
    return jnp.concatenate([prev_p, prev_s.reshape(-1, x.shape[1])], axis=0)


def _pair_states(s):
    bsz, n_heads = s.shape[:2]
    st = jnp.swapaxes(s, -1, -2).reshape(bsz, n_heads // 2, 2, HEAD_DIM, HEAD_DIM)
    z = jnp.zeros_like(st[:, :, 0])
    top = jnp.concatenate([st[:, :, 0], z], axis=-1)
    bot = jnp.concatenate([z, st[:, :, 1]], axis=-1)
    return jnp.concatenate([top, bot], axis=-2)


def _unpair_states(sp):
    bsz, n_pairs = sp.shape[:2]
    h0 = sp[:, :, :HEAD_DIM, :HEAD_DIM]
    h1 = sp[:, :, HEAD_DIM:, HEAD_DIM:]
    st = jnp.stack([h0, h1], axis=2).reshape(bsz, 2 * n_pairs, HEAD_DIM, HEAD_DIM)
    return jnp.swapaxes(st, -1, -2)


def kernel(x_prompt, x_sample, cache_sb_k, cache_sb_v, state_shift, state_wkv, state_conv, page_table, norm_mix, w_in, mu_rkv, mu_x, w0, w1, w2, a0, a1, a2, g1, g2, mu_vr, v0, v1, v2, k_k, k_a, r_k, ln_x_w, ln_x_b, w_a, sb_bias, w_b, w_o, norm_ffn, w_up, w_gate, conv_w, conv_b, w_down, norm_final):
    depth = w_in.shape[0]
    bp, seq, d_model = x_prompt.shape
    bd, dec_seq, _ = x_sample.shape
    assert bp == 1, "one prompt sequence is concatenated with the sample rows"
    n_pool = cache_sb_k.shape[1]
    n_heads_a = state_wkv.shape[2]
    d_a = n_heads_a * HEAD_DIM
    n_heads_b = cache_sb_k.shape[3]
    d_b = n_heads_b * HEAD_DIM
    d_ff = w_up.shape[2]
    n_p = bp * seq
    n_s = bd * dec_seq
    n_pages = page_table.shape[1]
    assert n_heads_b * dec_seq == LANES, "sample queries of all heads fill one 128-row tile"

    pool_k = jnp.transpose(cache_sb_k, (0, 1, 3, 4, 2)).reshape(depth * n_pool, d_b, PAGE_SIZE)
    pool_v = jnp.transpose(cache_sb_v, (0, 1, 3, 4, 2)).reshape(depth * n_pool, d_b, PAGE_SIZE)
    pt_flat = page_table.reshape(-1).astype(jnp.int32)

    x = jnp.concatenate([x_prompt.reshape(n_p, d_model), x_sample.reshape(n_s, d_model)], axis=0)
    v_first = None
    outs = []
    for l in range(depth):
        w_in_b = w_in[l].astype(BF16)
        xn = _rms_norm(x, norm_mix[l])
        xn_b = xn.astype(BF16)
        xx = _shift_tokens(xn, n_p, dec_seq, state_shift[l]) - xn

        c0 = 3 * d_a
        p_rkv = matmul(xn_b, w_in_b[:, :c0])
        q_s = matmul(xn_b, w_in_b[:, c0:c0 + d_b])
        k_s = matmul(xn_b, w_in_b[:, c0 + d_b:c0 + 2 * d_b])
        v_s = matmul(xn_b, w_in_b[:, c0 + 2 * d_b:c0 + 3 * d_b])
        p_gate = matmul(xn_b, w_in_b[:, c0 + 3 * d_b:])
        prev_rkv = matmul(state_shift[l].astype(BF16), w_in_b[:, :c0])
        rkv = p_rkv + (_shift_tokens(p_rkv, n_p, dec_seq, prev_rkv) - p_rkv) * mu_rkv[l]
        r, k, v = rkv[:, :d_a], rkv[:, d_a:2 * d_a], rkv[:, 2 * d_a:]

        firsts = [w1[l], a1[l], g1[l]]
        mus = [mu_x[l, 0], mu_x[l, 1], mu_x[l, 2]]
        seconds = [w2[l], a2[l], g2[l]]
        if l > 0:
            firsts.append(v1[l - 1])
            mus.append(mu_vr[l - 1])
            seconds.append(v2[l - 1])
        widths = [f.shape[1] for f in firsts]
        tot = sum(widths)
        pad = (-tot) % LANES
        top = jnp.concatenate(firsts + [jnp.zeros((d_model, pad), F32)], axis=1)
        bot = jnp.concatenate([m[:, None] * f for m, f in zip(mus, firsts)]
                              + [jnp.zeros((d_model, pad), F32)], axis=1)
        w_l1 = jnp.concatenate([top, bot], axis=0).astype(BF16)
        h1 = matmul(jnp.concatenate([xn_b, xx.astype(BF16)], axis=1), w_l1)
        offs = [0]
        for wd in widths:
            offs.append(offs[-1] + wd)
        parts = [jnp.tanh(h1[:, offs[0]:offs[1]]), h1[:, offs[1]:offs[2]],
                 jax.nn.sigmoid(h1[:, offs[2]:offs[3]])]
        if l > 0:
            parts.append(h1[:, offs[3]:offs[4]])
        parts.append(jnp.zeros((h1.shape[0], pad), F32))
        h2 = jnp.concatenate(parts, axis=1).astype(BF16)
        w_l2 = jnp.zeros((tot + pad, len(seconds) * d_a), F32)
        for i, s2 in enumerate(seconds):
            w_l2 = w_l2.at[offs[i]:offs[i + 1], i * d_a:(i + 1) * d_a].set(s2)
        o2 = matmul(h2, w_l2.astype(BF16))
        w_log = -jax.nn.softplus(-(w0[l] + o2[:, :d_a])) - 0.5
        log_decay = -jnp.exp(w_log)
        a_rate = jax.nn.sigmoid(a0[l] + o2[:, d_a:2 * d_a])
        g_out = o2[:, 2 * d_a:3 * d_a]
        if l == 0:
            v_first = v
        else:
            v = v + (v_first - v) * jax.nn.sigmoid(v0[l - 1] + o2[:, 3 * d_a:4 * d_a])

        def hd(t):
            return t.reshape(t.shape[0], n_heads_a, HEAD_DIM)

        kk = hd(k * k_k[l])
        kk = (kk * lax.rsqrt(jnp.sum(kk * kk, axis=-1, keepdims=True) + 1e-12)).reshape(-1, d_a)
        k_mod = k * (1.0 + (a_rate - 1.0) * k_a[l])
        scan_in = (r, log_decay, k_mod, v, -kk, kk * a_rate)
        s0_p = jnp.zeros((bp, n_heads_a // 2, LANES, LANES), F32)
        y_p, sf_p = wkv7(*scan_in, s0_p, row0=0, n_seq=bp, n_chunks=seq // WKV_CHUNK, c_len=WKV_CHUNK)
        y_s, sf_s = wkv7(*scan_in, _pair_states(state_wkv[l]), row0=n_p, n_seq=bd, n_chunks=1, c_len=dec_seq)
        y_h = hd(jnp.concatenate([y_p, y_s], axis=0))
        mu = jnp.mean(y_h, axis=-1, keepdims=True)
        var = jnp.mean(jnp.square(y_h - mu), axis=-1, keepdims=True)
        y_n = ((y_h - mu) * lax.rsqrt(var + GN_EPS)).reshape(-1, d_a) * ln_x_w[l] + ln_x_b[l]
        bonus = (jnp.sum(hd(r * k_mod) * r_k[l], axis=-1, keepdims=True) * hd(v)).reshape(-1, d_a)
        branch_a = matmul(((y_n + bonus) * g_out).astype(BF16), w_a[l].astype(BF16))

        q_b = (q_s * (HEAD_DIM ** -0.5)).astype(BF16)
        k_b = k_s.astype(BF16)
        v_b = v_s.astype(BF16)
        att_p = sb_prompt(q_b, k_b, v_b, sb_bias[l], seq_len=n_p)
        q_smp = q_b[n_p:].reshape(bd, dec_seq, n_heads_b, HEAD_DIM)
        head_eye = jnp.eye(n_heads_b, dtype=BF16)
        qbd = jnp.einsum('bthd,hg->bhtgd', q_smp, head_eye).reshape(bd, n_heads_b * dec_seq, d_b)
        bias_col = jnp.repeat(sb_bias[l].astype(F32), dec_seq)[:, None]
        pad_rows = ((0, 0), (0, PAGE_SIZE - dec_seq), (0, 0))
        k_new = jnp.pad(k_b[n_p:].reshape(bd, dec_seq, d_b), pad_rows)
        v_new = jnp.pad(v_b[n_p:].reshape(bd, dec_seq, d_b), pad_rows)
        att_s = sb_sample(qbd, bias_col, k_new, v_new, pool_k, pool_v, pt_flat + l * n_pool,
                          dec_seq=dec_seq, n_heads=n_heads_b)
        att = jnp.concatenate([att_p, att_s.reshape(n_s, d_b)], axis=0)
        branch_b = matmul(att, w_b[l].astype(BF16))

        g_a, g_b = p_gate[:, :d_model], p_gate[:, d_model:]
        merged = jax.nn.sigmoid(g_a) * branch_a + jax.nn.sigmoid(g_b) * branch_b
        x = x + matmul(merged.astype(BF16), w_o[l].astype(BF16))

        xf = _rms_norm(x, norm_ffn[l]).astype(BF16)
        u = matmul(xf, w_up[l].astype(BF16))
        gate = matmul(xf, w_gate[l].astype(BF16))
        conv_first = jnp.zeros((bp, 2, d_ff), F32)
        g_p = jnp.concatenate([conv_first[0], gate[:n_p]], axis=0)
        g_sm = jnp.concatenate([state_conv[l], gate[n_p:].reshape(bd, dec_seq, d_ff)], axis=1)
        cw = conv_w[l]
        conv_p = cw[0] * g_p[:-2] + cw[1] * g_p[1:-1] + cw[2] * g_p[2:] + conv_b[l]
        conv_s = cw[0] * g_sm[:, :-2] + cw[1] * g_sm[:, 1:-1] + cw[2] * g_sm[:, 2:] + conv_b[l]
        conv = jnp.concatenate([conv_p, conv_s.reshape(n_s, d_ff)], axis=0)
        x = x + matmul((jax.nn.silu(conv) * u).astype(BF16), w_down[l].astype(BF16))

        outs.append(dict(
            k_p=k_s[:n_p].reshape(bp, seq, n_heads_b, HEAD_DIM),
            v_p=v_s[:n_p].reshape(bp, seq, n_heads_b, HEAD_DIM),
            shift_p=xn[n_p - 1:n_p],
            wkv_p=_unpair_states(sf_p),
            conv_p=g_p[-2:][None],
            k_s=k_s[n_p:].reshape(bd, dec_seq, n_heads_b, HEAD_DIM),
            v_s=v_s[n_p:].reshape(bd, dec_seq, n_heads_b, HEAD_DIM),
            shift_s=xn[n_p:].reshape(bd, dec_seq, d_model)[:, -1],
            wkv_s=_unpair_states(sf_s),
            conv_s=g_sm[:, -2:]))

    y = _rms_norm(x, norm_final)

    def stack(name):
        return jnp.stack([o[name] for o in outs])

    return (y[:n_p].reshape(bp, seq, d_model), y[n_p:].reshape(bd, dec_seq, d_model),
            stack('shift_p'), stack('wkv_p'), stack('conv_p'), stack('k_p'), stack('v_p'),
            stack('shift_s'), stack('wkv_s'), stack('conv_s'), stack('k_s'), stack('v_s'))
```

```python
import functools

import jax
import jax.numpy as jnp
from jax import lax
from jax.experimental import pallas as pl
from jax.experimental.pallas import tpu as pltpu

F32 = jnp.float32
BF16 = jnp.bfloat16

HEAD_DIM = 64
LANES = 128
PAGE_SIZE = 128
RMS_EPS = 1e-6
GN_EPS = 64e-5
LOG2E = 1.4426950408889634
WKV_CHUNK = 64
VMEM_LIMIT = 48 * 1024 * 1024

NN = (((1,), (0,)), ((), ()))
NT = (((1,), (1,)), ((), ()))
TN = (((0,), (0,)), ((), ()))


def _dot(a, b, dims=NN):
    return lax.dot_general(a, b, dims, preferred_element_type=F32)


def _split2(x):
    hi = x.astype(BF16)
    lo = (x - hi.astype(F32)).astype(BF16)
    return hi, lo


def _dot3(a, b, dims=NN):
    ah, al = _split2(a)
    bh, bl = _split2(b)
    return _dot(ah, bh, dims) + (_dot(ah, bl, dims) + _dot(al, bh, dims))


def _pick(dim, candidates):
    for c in candidates:
        if dim % c == 0:
            return c
    return dim


def _mm_kernel_single(x_ref, w_ref, o_ref):
    o_ref[...] = _dot(x_ref[...], w_ref[...]).astype(o_ref.dtype)


def _mm_kernel_acc(x_ref, w_ref, o_ref, acc_ref, *, nk):
    k = pl.program_id(2)

    @pl.when(k == 0)
    def _():
        acc_ref[...] = jnp.zeros_like(acc_ref)

    acc_ref[...] += _dot(x_ref[...], w_ref[...])

    @pl.when(k == nk - 1)
    def _():
        o_ref[...] = acc_ref[...].astype(o_ref.dtype)


def matmul(x, w, out_dtype=F32):
    m, kd = x.shape
    n = w.shape[1]
    tm = _pick(m, (768, 512, 256, 128, 64, 32, 16, 8))
    tn = _pick(n, (1024, 512, 256, 128))
    tk = _pick(kd, (2048, 1024, 512, 256, 128))
    nk = kd // tk
    grid = (m // tm, n // tn, nk)
    params = pltpu.CompilerParams(
        dimension_semantics=("parallel", "parallel", "arbitrary"), vmem_limit_bytes=VMEM_LIMIT)
    in_specs = [pl.BlockSpec((tm, tk), lambda i, j, k: (i, k)),
                pl.BlockSpec((tk, tn), lambda i, j, k: (k, j))]
    out_spec = pl.BlockSpec((tm, tn), lambda i, j, k: (i, j))
    if nk == 1:
        return pl.pallas_call(
            _mm_kernel_single, grid=grid, in_specs=in_specs, out_specs=out_spec,
            out_shape=jax.ShapeDtypeStruct((m, n), out_dtype), compiler_params=params,
            name="mm")(x, w)
    return pl.pallas_call(
        functools.partial(_mm_kernel_acc, nk=nk), grid=grid, in_specs=in_specs, out_specs=out_spec,
        out_shape=jax.ShapeDtypeStruct((m, n), out_dtype),
        scratch_shapes=[pltpu.VMEM((tm, tn), F32)], compiler_params=params, name="mm_acc")(x, w)


def _bf(x):
    return x.astype(BF16)


def _wkv_chunk_terms(rs, lws, ks, vs, as_, bs, c_len):
    n = len(rs)
    c2 = 2 * c_len
    row = lax.broadcasted_iota(jnp.int32, (c_len, c_len), 0)
    col = lax.broadcasted_iota(jnp.int32, (c_len, c_len), 1)
    eye = (col == row).astype(F32)
    tri = (col <= row).astype(BF16)
    r2 = lax.broadcasted_iota(jnp.int32, (c2, c2), 0)
    j2 = lax.broadcasted_iota(jnp.int32, (c2, c2), 1) & (c_len - 1)
    t2 = r2 & (c_len - 1)
    keep = jnp.logical_or(j2 < t2, jnp.logical_and(r2 >= c_len, j2 == t2))
    lane = lax.broadcasted_iota(jnp.int32, (c_len, LANES), 1)
    m0 = lane < HEAD_DIM
    masks = (m0, jnp.logical_not(m0))
    rr = lax.broadcasted_iota(jnp.int32, (LANES, LANES), 0)
    cc = lax.broadcasted_iota(jnp.int32, (LANES, LANES), 1)
    same_head = (rr < HEAD_DIM) == (cc < HEAD_DIM)
    diag = rr == cc
    n_double = max(c_len.bit_length() - 2, 0)
    zeros = jnp.zeros((c_len, LANES), F32)

    ats, rts, qs, bks, wcs, vpads = [], [], [], [], [], []
    for p in range(n):
        lw = lws[p]
        l1 = _bf(lw)
        rem = lw - l1.astype(F32)
        l2 = _bf(rem)
        l3 = _bf(rem - l2.astype(F32))
        lam = _dot(tri, l1) + (_dot(tri, l2) + _dot(tri, l3))
        lam_c = lam[c_len - 1:c_len, :]
        e_neg = jnp.exp(-lam)
        e_end = jnp.exp(lam_c - lam)
        ats.append(as_[p] * jnp.exp(lam - lw))
        rts.append(rs[p] * jnp.exp(lam))
        qs.append(_bf(jnp.concatenate([bs[p] * e_neg, ks[p] * e_neg], axis=0)))
        bks.append(_bf(jnp.concatenate([bs[p] * e_end, ks[p] * e_end], axis=0)))
        wcs.append(jnp.exp(lam_c))
        vpads.append(_bf(jnp.concatenate([zeros, vs[p]], axis=0)))

    units = [(p, h) for p in range(n) for h in range(2)]
    ms = []
    for p, h in units:
        lhs = jnp.concatenate([jnp.where(masks[h], ats[p], 0.0), jnp.where(masks[h], rts[p], 0.0)], axis=0)
        ms.append(jnp.where(keep, _dot(_bf(lhs), qs[p], NT), 0.0))
    tops = [_bf(m[:c_len]) for m in ms]
    bots = [_bf(m[c_len:]) for m in ms]
    pws = [m[:c_len, :c_len] for m in ms]
    tinvs = [eye + x for x in pws]
    for _ in range(n_double):
        pws = [_dot(_bf(x), _bf(x)) for x in pws]
        tinvs = [t + _dot(_bf(t), _bf(x)) for t, x in zip(tinvs, pws)]
    tinvs = [_bf(t) for t in tinvs]
    at_b = [_bf(x) for x in ats]
    tas = [_dot(t, at_b[p]) for t, (p, h) in zip(tinvs, units)]
    lakv = [_dot(top, vpads[p]) for top, (p, h) in zip(tops, units)]
    u0s = [_dot(t, _bf(x)) for t, x in zip(tinvs, lakv)]
    ta_p = [_bf(jnp.where(m0, tas[2 * p], tas[2 * p + 1])) for p in range(n)]
    u0_p = [jnp.where(m0, u0s[2 * p], u0s[2 * p + 1]) for p in range(n)]
    uv_p = [_bf(jnp.concatenate([u0_p[p], vs[p]], axis=0)) for p in range(n)]
    rcs = [_dot(bot[:, :c_len], ta_p[p]) for bot, (p, h) in zip(bots, units)]
    y0s = [_dot(bot, uv_p[p]) for bot, (p, h) in zip(bots, units)]
    rc = [rts[p] + jnp.where(m0, rcs[2 * p], rcs[2 * p + 1]) for p in range(n)]
    y0 = [jnp.where(m0, y0s[2 * p], y0s[2 * p + 1]) for p in range(n)]
    pc = [jnp.where(same_head, _dot(bks[p][:c_len], ta_p[p], TN), 0.0)
          + jnp.where(diag, jnp.broadcast_to(wcs[p], (LANES, LANES)), 0.0) for p in range(n)]
    d0 = [jnp.where(same_head, _dot(bks[p], uv_p[p], TN), 0.0) for p in range(n)]
    return rc, y0, pc, d0


def _wkv_terms_kernel(r_ref, lw_ref, k_ref, v_ref, a_ref, b_ref, rc_ref, y0_ref, pc_ref, d0_ref,
                      *, c_len, pairs):
    sls = [slice(p * LANES, (p + 1) * LANES) for p in range(pairs)]
    rc, y0, pc, d0 = _wkv_chunk_terms(*[[ref[:, sl] for sl in sls]
                                        for ref in (r_ref, lw_ref, k_ref, v_ref, a_ref, b_ref)], c_len)
    for p, sl in enumerate(sls):
        rc_ref[:, sl] = rc[p]
        y0_ref[:, sl] = y0[p]
        pc_ref[0, p] = pc[p]
        d0_ref[0, p] = d0[p]


def _wkv_sweep_kernel(rc_ref, y0_ref, pc_ref, d0_ref, s0_ref, y_ref, sout_ref, st_ref, *, n_chunks, n_pairs):
    c = pl.program_id(1)

    @pl.when(c == 0)
    def _():
        st_ref[...] = s0_ref[0]

    for p in range(n_pairs):
        sl = slice(p * LANES, (p + 1) * LANES)
        st = st_ref[p]
        y_ref[:, sl] = _dot3(rc_ref[:, sl], st) + y0_ref[:, sl]
        st_ref[p] = _dot3(pc_ref[0, p], st) + d0_ref[0, p]

    @pl.when(c == n_chunks - 1)
    def _():
        sout_ref[0] = st_ref[...]


def wkv7(r, lw, k, v, a, b, s0, *, row0, n_seq, n_chunks, c_len):
    width = r.shape[1]
    n_pairs = width // LANES
    pairs = n_pairs
    tot_chunks = n_seq * n_chunks
    rows = tot_chunks * c_len
    blk0 = row0 // c_len
    lw_cols = pairs * LANES
    in_spec = pl.BlockSpec((c_len, lw_cols), lambda c, g: (blk0 + c, g))
    row_spec = pl.BlockSpec((c_len, lw_cols), lambda c, g: (c, g))
    mat_spec = pl.BlockSpec((1, pairs, LANES, LANES), lambda c, g: (c, g, 0, 0))
    rc, y0, pc, d0 = pl.pallas_call(
        functools.partial(_wkv_terms_kernel, c_len=c_len, pairs=pairs),
        grid=(tot_chunks, n_pairs // pairs),
        in_specs=[in_spec] * 6,
        out_specs=[row_spec, row_spec, mat_spec, mat_spec],
        out_shape=[jax.ShapeDtypeStruct((rows, width), F32), jax.ShapeDtypeStruct((rows, width), F32),
                   jax.ShapeDtypeStruct((tot_chunks, n_pairs, LANES, LANES), F32),
                   jax.ShapeDtypeStruct((tot_chunks, n_pairs, LANES, LANES), F32)],
        compiler_params=pltpu.CompilerParams(
            dimension_semantics=("parallel", "parallel"), vmem_limit_bytes=VMEM_LIMIT),
        name="wkv_terms")(r, lw, k, v, a, b)

    rowp = pl.BlockSpec((c_len, width), lambda s, c: (s * n_chunks + c, 0))
    matp = pl.BlockSpec((1, n_pairs, LANES, LANES), lambda s, c: (s * n_chunks + c, 0, 0, 0))
    stp = pl.BlockSpec((1, n_pairs, LANES, LANES), lambda s, c: (s, 0, 0, 0))
    y, s_fin = pl.pallas_call(
        functools.partial(_wkv_sweep_kernel, n_chunks=n_chunks, n_pairs=n_pairs),
        grid=(n_seq, n_chunks),
        in_specs=[rowp, rowp, matp, matp, stp],
        out_specs=[rowp, stp],
        out_shape=[jax.ShapeDtypeStruct((rows, width), F32),
                   jax.ShapeDtypeStruct((n_seq, n_pairs, LANES, LANES), F32)],
        scratch_shapes=[pltpu.VMEM((n_pairs, LANES, LANES), F32)],
        compiler_params=pltpu.CompilerParams(
            dimension_semantics=("parallel", "arbitrary"), vmem_limit_bytes=VMEM_LIMIT),
        name="wkv_sweep")(rc, y0, pc, d0, s0)
    return y, s_fin


def _tri_ones():
    jr = lax.broadcasted_iota(jnp.int32, (2 * LANES, 2 * LANES), 0) & (LANES - 1)
    jc = lax.broadcasted_iota(jnp.int32, (2 * LANES, 2 * LANES), 1)
    return jnp.logical_or(jc >= LANES, jr >= jc).astype(BF16)


def _sb_sweep(zl, valid, carry, tri_ones):
    n_sub = zl.shape[1] // LANES
    atts = [None] * n_sub
    for s in reversed(range(n_sub)):
        sl = slice(s * LANES, (s + 1) * LANES)
        z_s = zl[:, sl]
        sp = jnp.maximum(z_s, 0.0) + jnp.log(1.0 + jnp.exp2(-jnp.abs(z_s))) * LOG2E
        if valid is not None:
            sp = jnp.where(valid[:, sl], sp, 0.0)
        ext = _dot(jnp.concatenate(_split2(sp), axis=1), tri_ones)
        att = jnp.exp2(z_s - (ext[:, :LANES] + carry))
        if valid is not None:
            att = jnp.where(valid[:, sl], att, 0.0)
        atts[s] = att.astype(BF16)
        carry = carry + ext[:, LANES:]
    return (jnp.concatenate(atts, axis=1) if n_sub > 1 else atts[0]), carry


def _sb_prompt_kernel(q_ref, bias_ref, k_ref, v_ref, o_ref, *, tq, kb):
    i = pl.program_id(1)
    q = q_ref[...]
    lane = lax.broadcasted_iota(jnp.int32, (tq, LANES), 1)
    m0 = lane < HEAD_DIM
    zero = jnp.zeros_like(q)
    qs = jnp.concatenate([jnp.where(m0, q, zero), jnp.where(m0, zero, q)], axis=0)
    bias = bias_ref[0]
    tri_ones = _tri_ones()

    def logits(j):
        return _dot(qs, k_ref[pl.ds(pl.multiple_of(j * kb, kb), kb), :], NT) + bias

    def weighted(att, j):
        return _dot(att, v_ref[pl.ds(pl.multiple_of(j * kb, kb), kb), :])

    jd = (i * tq) // kb
    row = lax.broadcasted_iota(jnp.int32, (2 * tq, kb), 0)
    col = lax.broadcasted_iota(jnp.int32, (2 * tq, kb), 1)
    valid = (jd * kb + col) < i * tq + jnp.where(row >= tq, row - tq, row)
    att, carry = _sb_sweep(logits(jd), valid, jnp.zeros((2 * tq, LANES), F32), tri_ones)

    def body(jj, state):
        acc, carry, zl, att_prev = state
        j = jd - 1 - jj
        acc = acc + weighted(att_prev, j + 1)
        zl_next = logits(jnp.maximum(j - 1, 0))
        att, carry = _sb_sweep(zl, None, carry, tri_ones)
        return acc, carry, zl_next, att

    acc, carry, _, att = lax.fori_loop(
        0, jd, body, (jnp.zeros((2 * tq, LANES), F32), carry, logits(jnp.maximum(jd - 1, 0)), att))
    acc = acc + weighted(att, 0)
    o_ref[...] = jnp.where(m0, acc[:tq], acc[tq:]).astype(o_ref.dtype)


def sb_prompt(q, k, v, bias, *, seq_len, tq=128, kb=512):
    width = q.shape[1]
    n_pairs = width // LANES
    nq = seq_len // tq
    assert kb % tq == 0 and seq_len % kb == 0
    bias_cols = jnp.repeat(bias.astype(F32).reshape(n_pairs, 2), tq, axis=1).reshape(n_pairs, 2 * tq, 1)
    return pl.pallas_call(
        functools.partial(_sb_prompt_kernel, tq=tq, kb=kb),
        grid=(n_pairs, nq),
        in_specs=[pl.BlockSpec((tq, LANES), lambda p, i: (i, p)),
                  pl.BlockSpec((1, 2 * tq, 1), lambda p, i: (p, 0, 0)),
                  pl.BlockSpec((seq_len, LANES), lambda p, i: (0, p)),
                  pl.BlockSpec((seq_len, LANES), lambda p, i: (0, p))],
        out_specs=pl.BlockSpec((tq, LANES), lambda p, i: (i, p)),
        out_shape=jax.ShapeDtypeStruct((seq_len, width), BF16),
        compiler_params=pltpu.CompilerParams(
            dimension_semantics=("parallel", "arbitrary"), vmem_limit_bytes=VMEM_LIMIT),
        name="sb_prompt")(q, bias_cols, k, v)


def _sb_sample_kernel(pt_ref, qbd_ref, bias_ref, kn_ref, vn_ref, *rest, n_steps, dec_seq, n_heads, n_pg):
    kp_refs, vp_refs = rest[:n_pg], rest[n_pg:2 * n_pg]
    o_ref, acc_ref, carry_ref = rest[2 * n_pg:]
    j = pl.program_id(1)
    rows = qbd_ref.shape[1]
    tri_ones = _tri_ones()
    qbd = qbd_ref[0]
    bias = bias_ref[...]

    @pl.when(j == 0)
    def _():
        row = lax.broadcasted_iota(jnp.int32, (rows, PAGE_SIZE), 0)
        col = lax.broadcasted_iota(jnp.int32, (rows, PAGE_SIZE), 1)
        zl = _dot(qbd, kn_ref[0], NT) + bias
        att, carry = _sb_sweep(zl, col < row % dec_seq, jnp.zeros((rows, LANES), F32), tri_ones)
        acc_ref[...] = _dot(att, vn_ref[0])
        carry_ref[...] = carry

    @pl.when(j > 0)
    def _():
        kt = jnp.concatenate([kp_refs[i][0].astype(BF16) for i in range(n_pg)], axis=1)
        vt = jnp.concatenate([vp_refs[i][0].astype(BF16) for i in range(n_pg)], axis=1)
        att, carry = _sb_sweep(_dot(qbd, kt) + bias, None, carry_ref[...], tri_ones)
        acc_ref[...] += _dot(att, vt, NT)
        carry_ref[...] = carry

    @pl.when(j == n_steps - 1)
    def _():
        lane = lax.broadcasted_iota(jnp.int32, (dec_seq, n_heads * HEAD_DIM), 1)
        out = jnp.zeros((dec_seq, n_heads * HEAD_DIM), F32)
        for h in range(n_heads):
            blk = acc_ref[h * dec_seq:(h + 1) * dec_seq, :]
            out = jnp.where(lane // HEAD_DIM == h, blk, out)
        o_ref[0] = out.astype(o_ref.dtype)


def sb_sample(qbd, bias_col, k_new, v_new, pool_kt, pool_vt, page_ids, *, dec_seq, n_heads, n_pg=8):
    bsz, rows, width = qbd.shape
    n_pages = page_ids.shape[0] // bsz
    assert n_pages % n_pg == 0
    n_steps = n_pages // n_pg + 1

    def pool_idx(i):
        return lambda b, j, pt: (pt[b * n_pages + n_pages - n_pg * jnp.maximum(j, 1) + i], 0, 0)

    page_specs = [pl.BlockSpec((1, width, PAGE_SIZE), pool_idx(i)) for i in range(n_pg)]
    grid_spec = pltpu.PrefetchScalarGridSpec(
        num_scalar_prefetch=1, grid=(bsz, n_steps),
        in_specs=[pl.BlockSpec((1, rows, width), lambda b, j, pt: (b, 0, 0)),
                  pl.BlockSpec((rows, 1), lambda b, j, pt: (0, 0)),
                  pl.BlockSpec((1, PAGE_SIZE, width), lambda b, j, pt: (b, 0, 0)),
                  pl.BlockSpec((1, PAGE_SIZE, width), lambda b, j, pt: (b, 0, 0))] + page_specs + page_specs,
        out_specs=pl.BlockSpec((1, dec_seq, width), lambda b, j, pt: (b, 0, 0)),
        scratch_shapes=[pltpu.VMEM((rows, width), F32), pltpu.VMEM((rows, PAGE_SIZE), F32)])
    return pl.pallas_call(
        functools.partial(_sb_sample_kernel, n_steps=n_steps, dec_seq=dec_seq, n_heads=n_heads, n_pg=n_pg),
        grid_spec=grid_spec,
        out_shape=jax.ShapeDtypeStruct((bsz, dec_seq, width), BF16),
        compiler_params=pltpu.CompilerParams(
            dimension_semantics=("parallel", "arbitrary"), vmem_limit_bytes=VMEM_LIMIT),
        name="sb_sample")(page_ids, qbd, bias_col, k_new, v_new, *([pool_kt] * n_pg), *([pool_vt] * n_pg))


def _rms_norm(x, g):
    return x * lax.rsqrt(jnp.mean(x * x, axis=-1, keepdims=True) + RMS_EPS) * g


def _shift_tokens(x, n_prompt, dec_seq, first_sample):
    xp = x[:n_prompt]
    xs = x[n_prompt:].reshape(first_sample.shape[0], dec_seq, -1)
    prev_p = jnp.concatenate([jnp.zeros_like(xp[:1]), xp[:-1]], axis=0)
    prev_s = jnp.concatenate([first_sample[:, None], xs[:, :-1]], axis=1)
    return jnp.concatenate([prev_p, prev_s.reshape(-1, x.shape[1])], axis=0)


def _pair_states(s):
    bsz, n_heads = s.shape[:2]
    st = jnp.swapaxes(s, -1, -2).reshape(bsz, n_heads // 2, 2, HEAD_DIM, HEAD_DIM)
    z = jnp.zeros_like(st[:, :, 0])
    top = jnp.concatenate([st[:, :, 0], z], axis=-1)
    bot = jnp.concatenate([z, st[:, :, 1]], axis=-1)
    return jnp.concatenate([top, bot], axis=-2)


def _unpair_states(sp):
    bsz, n_pairs = sp.shape[:2]
    h0 = sp[:, :, :HEAD_DIM, :HEAD_DIM]
    h1 = sp[:, :, HEAD_DIM:, HEAD_DIM:]
    st = jnp.stack([h0, h1], axis=2).reshape(bsz, 2 * n_pairs, HEAD_DIM, HEAD_DIM)
    return jnp.swapaxes(st, -1, -2)


def kernel(x_prompt, x_sample, cache_sb_k, cache_sb_v, state_shift, state_wkv, state_conv, page_table, norm_mix, w_in, mu_rkv, mu_x, w0, w1, w2, a0, a1, a2, g1, g2, mu_vr, v0, v1, v2, k_k, k_a, r_k, ln_x_w, ln_x_b, w_a, sb_bias, w_b, w_o, norm_ffn, w_up, w_gate, conv_w, conv_b, w_down, norm_final):
    depth = w_in.shape[0]
    bp, seq, d_model = x_prompt.shape
    bd, dec_seq, _ = x_sample.shape
    assert bp == 1, "one prompt sequence is concatenated with the sample rows"
    n_pool = cache_sb_k.shape[1]
    n_heads_a = state_wkv.shape[2]
    d_a = n_heads_a * HEAD_DIM
    n_heads_b = cache_sb_k.shape[3]
    d_b = n_heads_b * HEAD_DIM
    d_ff = w_up.shape[2]
    n_p = bp * seq
    n_s = bd * dec_seq
    n_pages = page_table.shape[1]
    assert n_heads_b * dec_seq == LANES, "sample queries of all heads fill one 128-row tile"

    pool_k = jnp.transpose(cache_sb_k, (0, 1, 3, 4, 2)).reshape(depth * n_pool, d_b, PAGE_SIZE)
    pool_v = jnp.transpose(cache_sb_v, (0, 1, 3, 4, 2)).reshape(depth * n_pool, d_b, PAGE_SIZE)
    pt_flat = page_table.reshape(-1).astype(jnp.int32)

    x = jnp.concatenate([x_prompt.reshape(n_p, d_model), x_sample.reshape(n_s, d_model)], axis=0)
    v_first = None
    outs = []
    for l in range(depth):
        w_in_b = w_in[l].astype(BF16)
        xn = _rms_norm(x, norm_mix[l])
        xn_b = xn.astype(BF16)
        xx = _shift_tokens(xn, n_p, dec_seq, state_shift[l]) - xn

        c0 = 3 * d_a
        p_rkv = matmul(xn_b, w_in_b[:, :c0])
        q_s = matmul(xn_b, w_in_b[:, c0:c0 + d_b])
        k_s = matmul(xn_b, w_in_b[:, c0 + d_b:c0 + 2 * d_b])
        v_s = matmul(xn_b, w_in_b[:, c0 + 2 * d_b:c0 + 3 * d_b])
        p_gate = matmul(xn_b, w_in_b[:, c0 + 3 * d_b:])
        prev_rkv = matmul(state_shift[l].astype(BF16), w_in_b[:, :c0])
        rkv = p_rkv + (_shift_tokens(p_rkv, n_p, dec_seq, prev_rkv) - p_rkv) * mu_rkv[l]
        r, k, v = rkv[:, :d_a], rkv[:, d_a:2 * d_a], rkv[:, 2 * d_a:]

        firsts = [w1[l], a1[l], g1[l]]
        mus = [mu_x[l, 0], mu_x[l, 1], mu_x[l, 2]]
        seconds = [w2[l], a2[l], g2[l]]
        if l > 0:
            firsts.append(v1[l - 1])
            mus.append(mu_vr[l - 1])
            seconds.append(v2[l - 1])
        widths = [f.shape[1] for f in firsts]
        tot = sum(widths)
        pad = (-tot) % LANES
        top = jnp.concatenate(firsts + [jnp.zeros((d_model, pad), F32)], axis=1)
        bot = jnp.concatenate([m[:, None] * f for m, f in zip(mus, firsts)]
                              + [jnp.zeros((d_model, pad), F32)], axis=1)
        w_l1 = jnp.concatenate([top, bot], axis=0).astype(BF16)
        h1 = matmul(jnp.concatenate([xn_b, xx.astype(BF16)], axis=1), w_l1)
        offs = [0]
        for wd in widths:
            offs.append(offs[-1] + wd)
        parts = [jnp.tanh(h1[:, offs[0]:offs[1]]), h1[:, offs[1]:offs[2]],
                 jax.nn.sigmoid(h1[:, offs[2]:offs[3]])]
        if l > 0:
            parts.append(h1[:, offs[3]:offs[4]])
        parts.append(jnp.zeros((h1.shape[0], pad), F32))
        h2 = jnp.concatenate(parts, axis=1).astype(BF16)
        w_l2 = jnp.zeros((tot + pad, len(seconds) * d_a), F32)
        for i, s2 in enumerate(seconds):
            w_l2 = w_l2.at[offs[i]:offs[i + 1], i * d_a:(i + 1) * d_a].set(s2)
        o2 = matmul(h2, w_l2.astype(BF16))
        w_log = -jax.nn.softplus(-(w0[l] + o2[:, :d_a])) - 0.5
        log_decay = -jnp.exp(w_log)
        a_rate = jax.nn.sigmoid(a0[l] + o2[:, d_a:2 * d_a])
        g_out = o2[:, 2 * d_a:3 * d_a]
        if l == 0:
            v_first = v
        else:
            v = v + (v_first - v) * jax.nn.sigmoid(v0[l - 1] + o2[:, 3 * d_a:4 * d_a])

        def hd(t):
            return t.reshape(t.shape[0], n_heads_a, HEAD_DIM)

        kk = hd(k * k_k[l])
        kk = (kk * lax.rsqrt(jnp.sum(kk * kk, axis=-1, keepdims=True) + 1e-12)).reshape(-1, d_a)
        k_mod = k * (1.0 + (a_rate - 1.0) * k_a[l])
        scan_in = (r, log_decay, k_mod, v, -kk, kk * a_rate)
        s0_p = jnp.zeros((bp, n_heads_a // 2, LANES, LANES), F32)
        y_p, sf_p = wkv7(*scan_in, s0_p, row0=0, n_seq=bp, n_chunks=seq // WKV_CHUNK, c_len=WKV_CHUNK)
        y_s, sf_s = wkv7(*scan_in, _pair_states(state_wkv[l]), row0=n_p, n_seq=bd, n_chunks=1, c_len=dec_seq)
        y_h = hd(jnp.concatenate([y_p, y_s], axis=0))
        mu = jnp.mean(y_h, axis=-1, keepdims=True)
        var = jnp.mean(jnp.square(y_h - mu), axis=-1, keepdims=True)
        y_n = ((y_h - mu) * lax.rsqrt(var + GN_EPS)).reshape(-1, d_a) * ln_x_w[l] + ln_x_b[l]
        bonus = (jnp.sum(hd(r * k_mod) * r_k[l], axis=-1, keepdims=True) * hd(v)).reshape(-1, d_a)
        branch_a = matmul(((y_n + bonus) * g_out).astype(BF16), w_a[l].astype(BF16))

        q_b = (q_s * (LOG2E * HEAD_DIM ** -0.5)).astype(BF16)
        k_b = k_s.astype(BF16)
        v_b = v_s.astype(BF16)
        bias2 = sb_bias[l].astype(F32) * LOG2E
        att_p = sb_prompt(q_b, k_b, v_b, bias2, seq_len=n_p)
        q_smp = q_b[n_p:].reshape(bd, dec_seq, n_heads_b, HEAD_DIM)
        head_eye = jnp.eye(n_heads_b, dtype=BF16)
        qbd = jnp.einsum('bthd,hg->bhtgd', q_smp, head_eye).reshape(bd, n_heads_b * dec_seq, d_b)
        bias_col = jnp.repeat(bias2, dec_seq)[:, None]
        pad_rows = ((0, 0), (0, PAGE_SIZE - dec_seq), (0, 0))
        k_new = jnp.pad(k_b[n_p:].reshape(bd, dec_seq, d_b), pad_rows)
        v_new = jnp.pad(v_b[n_p:].reshape(bd, dec_seq, d_b), pad_rows)
        att_s = sb_sample(qbd, bias_col, k_new, v_new, pool_k, pool_v, pt_flat + l * n_pool,
                          dec_seq=dec_seq, n_heads=n_heads_b)
        att = jnp.concatenate([att_p, att_s.reshape(n_s, d_b)], axis=0)
        branch_b = matmul(att, w_b[l].astype(BF16))

        g_a, g_b = p_gate[:, :d_model], p_gate[:, d_model:]
        merged = jax.nn.sigmoid(g_a) * branch_a + jax.nn.sigmoid(g_b) * branch_b
        x = x + matmul(merged.astype(BF16), w_o[l].astype(BF16))

        xf = _rms_norm(x, norm_ffn[l]).astype(BF16)
        u = matmul(xf, w_up[l].astype(BF16))
        gate = matmul(xf, w_gate[l].astype(BF16))
        conv_first = jnp.zeros((bp, 2, d_ff), F32)
        g_p = jnp.concatenate([conv_first[0], gate[:n_p]], axis=0)
        g_sm = jnp.concatenate([state_conv[l], gate[n_p:].reshape(bd, dec_seq, d_ff)], axis=1)
        cw = conv_w[l]
        conv_p = cw[0] * g_p[:-2] + cw[1] * g_p[1:-1] + cw[2] * g_p[2:] + conv_b[l]
        conv_s = cw[0] * g_sm[:, :-2] + cw[1] * g_sm[:, 1:-1] + cw[2] * g_sm[:, 2:] + conv_b[l]
        conv = jnp.concatenate([conv_p, conv_s.reshape(n_s, d_ff)], axis=0)
        x = x + matmul((jax.nn.silu(conv) * u).astype(BF16), w_down[l].astype(BF16))

        outs.append(dict(
            k_p=k_s[:n_p].reshape(bp, seq, n_heads_b, HEAD_DIM),
            v_p=v_s[:n_p].reshape(bp, seq, n_heads_b, HEAD_DIM),
            shift_p=xn[n_p - 1:n_p],
            wkv_p=_unpair_states(sf_p),
            conv_p=g_p[-2:][None],
            k_s=k_s[n_p:].reshape(bd, dec_seq, n_heads_b, HEAD_DIM),
            v_s=v_s[n_p:].reshape(bd, dec_seq, n_heads_b, HEAD_DIM),
            shift_s=xn[n_p:].reshape(bd, dec_seq, d_model)[:, -1],
            wkv_s=_unpair_states(sf_s),
            conv_s=g_sm[:, -2:]))

    y = _rms_norm(x, norm_final)

    def stack(name):
        return jnp.stack([o[name] for o in outs])

    return (y[:n_p].reshape(bp, seq, d_model), y[n_p:].reshape(bd, dec_seq, d_model),
            stack('shift_p'), stack('wkv_p'), stack('conv_p'), stack('k_p'), stack('v_p'),
            stack('shift_s'), stack('wkv_s'), stack('conv_s'), stack('k_s'), stack('v_s'))
```

```python
import functools

import jax
import jax.numpy as jnp
from jax import lax
from jax.experimental import pallas as pl
from jax.experimental.pallas import tpu as pltpu

F32 = jnp.float32
BF16 = jnp.bfloat16

HEAD_DIM = 64
LANES = 128
PAGE_SIZE = 128
RMS_EPS = 1e-6
GN_EPS = 64e-5
LOG2E = 1.4426950408889634
WKV_CHUNK = 64
VMEM_LIMIT = 48 * 1024 * 1024

NN = (((1,), (0,)), ((), ()))
NT = (((1,), (1,)), ((), ()))
TN = (((0,), (0,)), ((), ()))


def _dot(a, b, dims=NN):
    return lax.dot_general(a, b, dims, preferred_element_type=F32)


def _split2(x):
    hi = x.astype(BF16)
    lo = (x - hi.astype(F32)).astype(BF16)
    return hi, lo


def _dot3(a, b, dims=NN):
    ah, al = _split2(a)
    bh, bl = _split2(b)
    return _dot(ah, bh, dims) + (_dot(ah, bl, dims) + _dot(al, bh, dims))


def _pick(dim, candidates):
    for c in candidates:
        if dim % c == 0:
            return c
    return dim


def _lora_act(h, bounds):
    col = lax.broadcasted_iota(jnp.int32, h.shape, 1)
    is_tanh = col < bounds[0]
    is_sig = jnp.logical_and(col >= bounds[1], col < bounds[2])
    return jnp.where(is_tanh, jnp.tanh(h), jnp.where(is_sig, jax.nn.sigmoid(h), h))


def _mm_kernel(x_ref, w_ref, *rest, nk, act):
    o_refs = rest[:-1] if nk > 1 else rest
    k = pl.program_id(2)

    def finish(acc):
        if act is not None:
            acc = _lora_act(acc, act)
        for o_ref in o_refs:
            o_ref[...] = acc.astype(o_ref.dtype)

    if nk == 1:
        finish(_dot(x_ref[...], w_ref[...]))
        return
    acc_ref = rest[-1]

    @pl.when(k == 0)
    def _():
        acc_ref[...] = jnp.zeros_like(acc_ref)

    acc_ref[...] += _dot(x_ref[...], w_ref[...])

    @pl.when(k == nk - 1)
    def _():
        finish(acc_ref[...])


def matmul(x, w, out_dtypes=(F32,), act=None):
    m = x.shape[0]
    kd, n = w.shape
    tm = _pick(m, (768, 512, 256, 128, 64, 32, 16, 8))
    tn = _pick(n, (1024, 512, 256, 128))
    tk = _pick(kd, (2048, 1024, 512, 256, 128))
    nk = kd // tk
    outs = pl.pallas_call(
        functools.partial(_mm_kernel, nk=nk, act=act),
        grid=(m // tm, n // tn, nk),
        in_specs=[pl.BlockSpec((tm, tk), lambda i, j, k: (i, k)),
                  pl.BlockSpec((tk, tn), lambda i, j, k: (k, j))],
        out_specs=[pl.BlockSpec((tm, tn), lambda i, j, k: (i, j))] * len(out_dtypes),
        out_shape=[jax.ShapeDtypeStruct((m, n), dt) for dt in out_dtypes],
        scratch_shapes=[pltpu.VMEM((tm, tn), F32)] if nk > 1 else [],
        compiler_params=pltpu.CompilerParams(
            dimension_semantics=("parallel", "parallel", "arbitrary"), vmem_limit_bytes=VMEM_LIMIT),
        name="mm")(x, w)
    return outs[0] if len(out_dtypes) == 1 else outs


def _bf(x):
    return x.astype(BF16)


def _wkv_chunk_terms(rs, lws, ks, vs, as_, bs, c_len):
    n = len(rs)
    c2 = 2 * c_len
    row = lax.broadcasted_iota(jnp.int32, (c_len, c_len), 0)
    col = lax.broadcasted_iota(jnp.int32, (c_len, c_len), 1)
    eye = (col == row).astype(F32)
    tri = (col <= row).astype(BF16)
    r2 = lax.broadcasted_iota(jnp.int32, (c2, c2), 0)
    j2 = lax.broadcasted_iota(jnp.int32, (c2, c2), 1) & (c_len - 1)
    t2 = r2 & (c_len - 1)
    keep = jnp.logical_or(j2 < t2, jnp.logical_and(r2 >= c_len, j2 == t2))
    lane = lax.broadcasted_iota(jnp.int32, (c_len, LANES), 1)
    m0 = lane < HEAD_DIM
    masks = (m0, jnp.logical_not(m0))
    rr = lax.broadcasted_iota(jnp.int32, (LANES, LANES), 0)
    cc = lax.broadcasted_iota(jnp.int32, (LANES, LANES), 1)
    same_head = (rr < HEAD_DIM) == (cc < HEAD_DIM)
    diag = rr == cc
    n_double = max(c_len.bit_length() - 2, 0)
    zeros = jnp.zeros((c_len, LANES), F32)

    ats, rts, qs, bks, wcs, vpads = [], [], [], [], [], []
    for p in range(n):
        lw = lws[p]
        l1 = _bf(lw)
        rem = lw - l1.astype(F32)
        l2 = _bf(rem)
        l3 = _bf(rem - l2.astype(F32))
        lam = _dot(tri, l1) + (_dot(tri, l2) + _dot(tri, l3))
        lam_c = lam[c_len - 1:c_len, :]
        e_neg = jnp.exp(-lam)
        e_end = jnp.exp(lam_c - lam)
        ats.append(as_[p] * jnp.exp(lam - lw))
        rts.append(rs[p] * jnp.exp(lam))
        qs.append(_bf(jnp.concatenate([bs[p] * e_neg, ks[p] * e_neg], axis=0)))
        bks.append(_bf(jnp.concatenate([bs[p] * e_end, ks[p] * e_end], axis=0)))
        wcs.append(jnp.exp(lam_c))
        vpads.append(_bf(jnp.concatenate([zeros, vs[p]], axis=0)))

    units = [(p, h) for p in range(n) for h in range(2)]
    ms = []
    for p, h in units:
        lhs = jnp.concatenate([jnp.where(masks[h], ats[p], 0.0), jnp.where(masks[h], rts[p], 0.0)], axis=0)
        ms.append(jnp.where(keep, _dot(_bf(lhs), qs[p], NT), 0.0))
    tops = [_bf(m[:c_len]) for m in ms]
    bots = [_bf(m[c_len:]) for m in ms]
    pws = [m[:c_len, :c_len] for m in ms]
    tinvs = [eye + x for x in pws]
    for _ in range(n_double):
        pws = [_dot(_bf(x), _bf(x)) for x in pws]
        tinvs = [t + _dot(_bf(t), _bf(x)) for t, x in zip(tinvs, pws)]
    tinvs = [_bf(t) for t in tinvs]
    at_b = [_bf(x) for x in ats]
    tas = [_dot(t, at_b[p]) for t, (p, h) in zip(tinvs, units)]
    lakv = [_dot(top, vpads[p]) for top, (p, h) in zip(tops, units)]
    u0s = [_dot(t, _bf(x)) for t, x in zip(tinvs, lakv)]
    ta_p = [_bf(jnp.where(m0, tas[2 * p], tas[2 * p + 1])) for p in range(n)]
    u0_p = [jnp.where(m0, u0s[2 * p], u0s[2 * p + 1]) for p in range(n)]
    uv_p = [_bf(jnp.concatenate([u0_p[p], vs[p]], axis=0)) for p in range(n)]
    rcs = [_dot(bot[:, :c_len], ta_p[p]) for bot, (p, h) in zip(bots, units)]
    y0s = [_dot(bot, uv_p[p]) for bot, (p, h) in zip(bots, units)]
    rc = [rts[p] + jnp.where(m0, rcs[2 * p], rcs[2 * p + 1]) for p in range(n)]
    y0 = [jnp.where(m0, y0s[2 * p], y0s[2 * p + 1]) for p in range(n)]
    pc = [jnp.where(same_head, _dot(bks[p][:c_len], ta_p[p], TN), 0.0)
          + jnp.where(diag, jnp.broadcast_to(wcs[p], (LANES, LANES)), 0.0) for p in range(n)]
    d0 = [jnp.where(same_head, _dot(bks[p], uv_p[p], TN), 0.0) for p in range(n)]
    return rc, y0, pc, d0


def _wkv_terms_kernel(r_ref, lw_ref, k_ref, v_ref, a_ref, b_ref, rc_ref, y0_ref, pc_ref, d0_ref,
                      *, c_len, pairs):
    sls = [slice(p * LANES, (p + 1) * LANES) for p in range(pairs)]
    rc, y0, pc, d0 = _wkv_chunk_terms(*[[ref[:, sl] for sl in sls]
                                        for ref in (r_ref, lw_ref, k_ref, v_ref, a_ref, b_ref)], c_len)
    for p, sl in enumerate(sls):
        rc_ref[:, sl] = rc[p]
        y0_ref[:, sl] = y0[p]
        pc_ref[0, p] = pc[p]
        d0_ref[0, p] = d0[p]


def _wkv_sweep_kernel(rc_ref, y0_ref, pc_ref, d0_ref, s0_ref, y_ref, sout_ref, st_ref, *, n_chunks, n_pairs):
    c = pl.program_id(1)

    @pl.when(c == 0)
    def _():
        st_ref[...] = s0_ref[0]

    for p in range(n_pairs):
        sl = slice(p * LANES, (p + 1) * LANES)
        st = st_ref[p]
        y_ref[:, sl] = _dot3(rc_ref[:, sl], st) + y0_ref[:, sl]
        st_ref[p] = _dot3(pc_ref[0, p], st) + d0_ref[0, p]

    @pl.when(c == n_chunks - 1)
    def _():
        sout_ref[0] = st_ref[...]


def wkv7(r, lw, k, v, a, b, s0, *, row0, n_seq, n_chunks, c_len, col_blocks=(0, 0, 0, 0, 0, 0)):
    width = lw.shape[1]
    n_pairs = width // LANES
    pairs = n_pairs
    tot_chunks = n_seq * n_chunks
    rows = tot_chunks * c_len
    blk0 = row0 // c_len
    lw_cols = pairs * LANES
    in_specs = [pl.BlockSpec((c_len, lw_cols), functools.partial(lambda c, g, cb: (blk0 + c, cb), cb=cb))
                for cb in col_blocks]
    row_spec = pl.BlockSpec((c_len, lw_cols), lambda c, g: (c, g))
    mat_spec = pl.BlockSpec((1, pairs, LANES, LANES), lambda c, g: (c, g, 0, 0))
    rc, y0, pc, d0 = pl.pallas_call(
        functools.partial(_wkv_terms_kernel, c_len=c_len, pairs=pairs),
        grid=(tot_chunks, n_pairs // pairs),
        in_specs=in_specs,
        out_specs=[row_spec, row_spec, mat_spec, mat_spec],
        out_shape=[jax.ShapeDtypeStruct((rows, width), F32), jax.ShapeDtypeStruct((rows, width), F32),
                   jax.ShapeDtypeStruct((tot_chunks, n_pairs, LANES, LANES), F32),
                   jax.ShapeDtypeStruct((tot_chunks, n_pairs, LANES, LANES), F32)],
        compiler_params=pltpu.CompilerParams(
            dimension_semantics=("parallel", "parallel"), vmem_limit_bytes=VMEM_LIMIT),
        name="wkv_terms")(r, lw, k, v, a, b)

    rowp = pl.BlockSpec((c_len, width), lambda s, c: (s * n_chunks + c, 0))
    matp = pl.BlockSpec((1, n_pairs, LANES, LANES), lambda s, c: (s * n_chunks + c, 0, 0, 0))
    stp = pl.BlockSpec((1, n_pairs, LANES, LANES), lambda s, c: (s, 0, 0, 0))
    y, s_fin = pl.pallas_call(
        functools.partial(_wkv_sweep_kernel, n_chunks=n_chunks, n_pairs=n_pairs),
        grid=(n_seq, n_chunks),
        in_specs=[rowp, rowp, matp, matp, stp],
        out_specs=[rowp, stp],
        out_shape=[jax.ShapeDtypeStruct((rows, width), F32),
                   jax.ShapeDtypeStruct((n_seq, n_pairs, LANES, LANES), F32)],
        scratch_shapes=[pltpu.VMEM((n_pairs, LANES, LANES), F32)],
        compiler_params=pltpu.CompilerParams(
            dimension_semantics=("parallel", "arbitrary"), vmem_limit_bytes=VMEM_LIMIT),
        name="wkv_sweep")(rc, y0, pc, d0, s0)
    return y, s_fin


def _tri_ones():
    jr = lax.broadcasted_iota(jnp.int32, (2 * LANES, 2 * LANES), 0) & (LANES - 1)
    jc = lax.broadcasted_iota(jnp.int32, (2 * LANES, 2 * LANES), 1)
    return jnp.logical_or(jc >= LANES, jr >= jc).astype(BF16)


def _sb_sweep(zl, valid, carry, tri_ones):
    n_sub = zl.shape[1] // LANES
    atts = [None] * n_sub
    for s in reversed(range(n_sub)):
        sl = slice(s * LANES, (s + 1) * LANES)
        z_s = zl[:, sl]
        sp = jnp.maximum(z_s, 0.0) + jnp.log(1.0 + jnp.exp2(-jnp.abs(z_s))) * LOG2E
        if valid is not None:
            sp = jnp.where(valid[:, sl], sp, 0.0)
        ext = _dot(jnp.concatenate(_split2(sp), axis=1), tri_ones)
        att = jnp.exp2(z_s - (ext[:, :LANES] + carry))
        if valid is not None:
            att = jnp.where(valid[:, sl], att, 0.0)
        atts[s] = att.astype(BF16)
        carry = carry + ext[:, LANES:]
    return (jnp.concatenate(atts, axis=1) if n_sub > 1 else atts[0]), carry


def _sb_prompt_kernel(q_ref, bias_ref, k_ref, v_ref, o_ref, *, tq, kb):
    i = pl.program_id(1)
    q = q_ref[...]
    lane = lax.broadcasted_iota(jnp.int32, (tq, LANES), 1)
    m0 = lane < HEAD_DIM
    zero = jnp.zeros_like(q)
    qs = jnp.concatenate([jnp.where(m0, q, zero), jnp.where(m0, zero, q)], axis=0)
    bias = bias_ref[0]
    tri_ones = _tri_ones()

    def logits(j):
        return _dot(qs, k_ref[pl.ds(pl.multiple_of(j * kb, kb), kb), :], NT) + bias

    def weighted(att, j):
        return _dot(att, v_ref[pl.ds(pl.multiple_of(j * kb, kb), kb), :])

    jd = (i * tq) // kb
    row = lax.broadcasted_iota(jnp.int32, (2 * tq, kb), 0)
    col = lax.broadcasted_iota(jnp.int32, (2 * tq, kb), 1)
    valid = (jd * kb + col) < i * tq + jnp.where(row >= tq, row - tq, row)
    att, carry = _sb_sweep(logits(jd), valid, jnp.zeros((2 * tq, LANES), F32), tri_ones)

    def body(jj, state):
        acc, carry, zl, att_prev = state
        j = jd - 1 - jj
        acc = acc + weighted(att_prev, j + 1)
        zl_next = logits(jnp.maximum(j - 1, 0))
        att, carry = _sb_sweep(zl, None, carry, tri_ones)
        return acc, carry, zl_next, att

    acc, carry, _, att = lax.fori_loop(
        0, jd, body, (jnp.zeros((2 * tq, LANES), F32), carry, logits(jnp.maximum(jd - 1, 0)), att))
    acc = acc + weighted(att, 0)
    o_ref[...] = jnp.where(m0, acc[:tq], acc[tq:]).astype(o_ref.dtype)


def sb_prompt(q, k, v, bias, *, seq_len, tq=128, kb=512):
    width = q.shape[1]
    n_pairs = width // LANES
    nq = seq_len // tq
    assert kb % tq == 0 and seq_len % kb == 0
    bias_cols = jnp.repeat(bias.astype(F32).reshape(n_pairs, 2), tq, axis=1).reshape(n_pairs, 2 * tq, 1)
    return pl.pallas_call(
        functools.partial(_sb_prompt_kernel, tq=tq, kb=kb),
        grid=(n_pairs, nq),
        in_specs=[pl.BlockSpec((tq, LANES), lambda p, i: (i, p)),
                  pl.BlockSpec((1, 2 * tq, 1), lambda p, i: (p, 0, 0)),
                  pl.BlockSpec((seq_len, LANES), lambda p, i: (0, p)),
                  pl.BlockSpec((seq_len, LANES), lambda p, i: (0, p))],
        out_specs=pl.BlockSpec((tq, LANES), lambda p, i: (i, p)),
        out_shape=jax.ShapeDtypeStruct((seq_len, width), BF16),
        compiler_params=pltpu.CompilerParams(
            dimension_semantics=("parallel", "arbitrary"), vmem_limit_bytes=VMEM_LIMIT),
        name="sb_prompt")(q, bias_cols, k, v)


def _sb_sample_kernel(pt_ref, qbd_ref, bias_ref, kn_ref, vn_ref, *rest, n_steps, dec_seq, n_heads, n_pg):
    kp_refs, vp_refs = rest[:n_pg], rest[n_pg:2 * n_pg]
    o_ref, acc_ref, carry_ref = rest[2 * n_pg:]
    j = pl.program_id(1)
    rows = qbd_ref.shape[1]
    tri_ones = _tri_ones()
    qbd = qbd_ref[0]
    bias = bias_ref[...]

    @pl.when(j == 0)
    def _():
        row = lax.broadcasted_iota(jnp.int32, (rows, PAGE_SIZE), 0)
        col = lax.broadcasted_iota(jnp.int32, (rows, PAGE_SIZE), 1)
        zl = _dot(qbd, kn_ref[0], NT) + bias
        att, carry = _sb_sweep(zl, col < row % dec_seq, jnp.zeros((rows, LANES), F32), tri_ones)
        acc_ref[...] = _dot(att, vn_ref[0])
        carry_ref[...] = carry

    @pl.when(j > 0)
    def _():
        kt = jnp.concatenate([kp_refs[i][0].astype(BF16) for i in range(n_pg)], axis=1)
        vt = jnp.concatenate([vp_refs[i][0].astype(BF16) for i in range(n_pg)], axis=1)
        att, carry = _sb_sweep(_dot(qbd, kt) + bias, None, carry_ref[...], tri_ones)
        acc_ref[...] += _dot(att, vt, NT)
        carry_ref[...] = carry

    @pl.when(j == n_steps - 1)
    def _():
        lane = lax.broadcasted_iota(jnp.int32, (dec_seq, n_heads * HEAD_DIM), 1)
        out = jnp.zeros((dec_seq, n_heads * HEAD_DIM), F32)
        for h in range(n_heads):
            blk = acc_ref[h * dec_seq:(h + 1) * dec_seq, :]
            out = jnp.where(lane // HEAD_DIM == h, blk, out)
        o_ref[0] = out.astype(o_ref.dtype)


def sb_sample(qbd, bias_col, k_new, v_new, pool_kt, pool_vt, page_ids, *, dec_seq, n_heads, n_pg=8):
    bsz, rows, width = qbd.shape
    n_pages = page_ids.shape[0] // bsz
    assert n_pages % n_pg == 0
    n_steps = n_pages // n_pg + 1

    def pool_idx(i):
        return lambda b, j, pt: (pt[b * n_pages + n_pages - n_pg * jnp.maximum(j, 1) + i], 0, 0)

    page_specs = [pl.BlockSpec((1, width, PAGE_SIZE), pool_idx(i)) for i in range(n_pg)]
    grid_spec = pltpu.PrefetchScalarGridSpec(
        num_scalar_prefetch=1, grid=(bsz, n_steps),
        in_specs=[pl.BlockSpec((1, rows, width), lambda b, j, pt: (b, 0, 0)),
                  pl.BlockSpec((rows, 1), lambda b, j, pt: (0, 0)),
                  pl.BlockSpec((1, PAGE_SIZE, width), lambda b, j, pt: (b, 0, 0)),
                  pl.BlockSpec((1, PAGE_SIZE, width), lambda b, j, pt: (b, 0, 0))] + page_specs + page_specs,
        out_specs=pl.BlockSpec((1, dec_seq, width), lambda b, j, pt: (b, 0, 0)),
        scratch_shapes=[pltpu.VMEM((rows, width), F32), pltpu.VMEM((rows, PAGE_SIZE), F32)])
    return pl.pallas_call(
        functools.partial(_sb_sample_kernel, n_steps=n_steps, dec_seq=dec_seq, n_heads=n_heads, n_pg=n_pg),
        grid_spec=grid_spec,
        out_shape=jax.ShapeDtypeStruct((bsz, dec_seq, width), BF16),
        compiler_params=pltpu.CompilerParams(
            dimension_semantics=("parallel", "arbitrary"), vmem_limit_bytes=VMEM_LIMIT),
        name="sb_sample")(page_ids, qbd, bias_col, k_new, v_new, *([pool_kt] * n_pg), *([pool_vt] * n_pg))


def _row_rms(x, g):
    return x * lax.rsqrt(jnp.mean(x * x, axis=-1, keepdims=True) + RMS_EPS) * g


def _norm_shift_kernel(x_ref, xp_ref, g_ref, first_ref, o_ref, *, n_prompt_tiles, dec_seq):
    i = pl.program_id(0)
    g = g_ref[...]
    xn = _row_rms(x_ref[...], g)
    tm, d = xn.shape
    above = _row_rms(xp_ref[...], g)[7:8]
    is_sample = i >= n_prompt_tiles
    first = jnp.where(is_sample, first_ref[...],
                      jnp.broadcast_to(above, xn.shape) * jnp.where(i > 0, 1.0, 0.0))
    row = lax.broadcasted_iota(jnp.int32, (tm, 1), 0)
    starts = (row & jnp.where(is_sample, dec_seq - 1, -1)) == 0
    prev = jnp.where(starts, first, pltpu.roll(xn, 1, 0))
    o_ref[:, :d] = xn.astype(BF16)
    o_ref[:, d:] = prev.astype(BF16)


def norm_shift(x, gain, first_rows, n_prompt, dec_seq):
    m, d = x.shape
    tm = first_rows.shape[0]
    assert n_prompt % tm == 0 and m == n_prompt + tm and dec_seq & (dec_seq - 1) == 0
    return pl.pallas_call(
        functools.partial(_norm_shift_kernel, n_prompt_tiles=n_prompt // tm, dec_seq=dec_seq),
        grid=(m // tm,),
        in_specs=[pl.BlockSpec((tm, d), lambda i: (i, 0)),
                  pl.BlockSpec((8, d), lambda i: (jnp.maximum(i * (tm // 8) - 1, 0), 0)),
                  pl.BlockSpec((1, d), lambda i: (0, 0)),
                  pl.BlockSpec((tm, d), lambda i: (0, 0))],
        out_specs=pl.BlockSpec((tm, 2 * d), lambda i: (i, 0)),
        out_shape=jax.ShapeDtypeStruct((m, 2 * d), BF16),
        compiler_params=pltpu.CompilerParams(dimension_semantics=("parallel",), vmem_limit_bytes=VMEM_LIMIT),
        name="norm_shift")(x, x, gain.reshape(1, d), first_rows)


def _seg_ones():
    r = lax.broadcasted_iota(jnp.int32, (2 * LANES, LANES), 0) & (LANES - 1)
    c = lax.broadcasted_iota(jnp.int32, (2 * LANES, LANES), 1)
    return ((r < HEAD_DIM) == (c < HEAD_DIM)).astype(BF16)


def _head_sums(x, seg):
    parts = [_dot(jnp.concatenate(_split2(x[:, t:t + LANES]), axis=1), seg)
             for t in range(0, x.shape[1], LANES)]
    return jnp.concatenate(parts, axis=1)


def _wkv_prep_kernel(rkv_ref, o2_ref, *rest, d_a, has_v):
    if has_v:
        vf_ref, p_ref, lw_ref, k_ref, a_ref, b_ref, v_ref = rest
    else:
        p_ref, lw_ref, k_ref, a_ref, b_ref = rest
    p = p_ref[...]
    seg = _seg_ones()
    k = rkv_ref[:, d_a:2 * d_a]
    dw = p[0:1] + o2_ref[:, :d_a]
    w_log = -(jnp.maximum(-dw, 0.0) + jnp.log(1.0 + jnp.exp(-jnp.abs(dw)))) - 0.5
    lw_ref[...] = -jnp.exp(w_log)
    a_rate = jax.nn.sigmoid(p[1:2] + o2_ref[:, d_a:2 * d_a])
    kk = k * p[2:3]
    kk = kk * lax.rsqrt(_head_sums(kk * kk, seg) + 1e-12)
    k_ref[...] = k * (1.0 + (a_rate - 1.0) * p[3:4])
    a_ref[...] = -kk
    b_ref[...] = kk * a_rate
    if has_v:
        v = rkv_ref[:, 2 * d_a:]
        v_ref[...] = v + (vf_ref[...] - v) * jax.nn.sigmoid(p[4:5] + o2_ref[:, 3 * d_a:])


def wkv_prep(rkv, o2, v_first_src, params, d_a, tm):
    m = rkv.shape[0]
    has_v = v_first_src is not None
    row = pl.BlockSpec((tm, d_a), lambda i: (i, 0))
    in_specs = [pl.BlockSpec((tm, rkv.shape[1]), lambda i: (i, 0)),
                pl.BlockSpec((tm, o2.shape[1]), lambda i: (i, 0))]
    args = [rkv, o2]
    if has_v:
        in_specs.append(pl.BlockSpec((tm, d_a), lambda i: (i, 2)))
        args.append(v_first_src)
    in_specs.append(pl.BlockSpec(params.shape, lambda i: (0, 0)))
    n_out = 5 if has_v else 4
    return pl.pallas_call(
        functools.partial(_wkv_prep_kernel, d_a=d_a, has_v=has_v),
        grid=(m // tm,), in_specs=in_specs, out_specs=[row] * n_out,
        out_shape=[jax.ShapeDtypeStruct((m, d_a), F32)] * n_out,
        compiler_params=pltpu.CompilerParams(dimension_semantics=("parallel",), vmem_limit_bytes=VMEM_LIMIT),
        name="wkv_prep")(*args, params)


def _wkv_post_kernel(y_ref, r_ref, k_ref, v_ref, g_ref, p_ref, o_ref):
    p = p_ref[...]
    seg = _seg_ones()
    y = y_ref[...]
    inv = 1.0 / HEAD_DIM
    d = y - _head_sums(y, seg) * inv
    var = _head_sums(d * d, seg) * inv
    y_n = d * lax.rsqrt(var + GN_EPS) * p[0:1] + p[1:2]
    bonus = _head_sums(r_ref[...] * k_ref[...] * p[2:3], seg) * v_ref[...]
    o_ref[...] = ((y_n + bonus) * g_ref[...]).astype(BF16)


def wkv_post(y, rkv, k_mod, v_src, v_col, o2, params, d_a, tm):
    m = y.shape[0]

    def col(c):
        return pl.BlockSpec((tm, d_a), lambda i: (i, c))

    return pl.pallas_call(
        _wkv_post_kernel, grid=(m // tm,),
        in_specs=[col(0), col(0), col(0), col(v_col), col(2), pl.BlockSpec(params.shape, lambda i: (0, 0))],
        out_specs=col(0), out_shape=jax.ShapeDtypeStruct((m, d_a), BF16),
        compiler_params=pltpu.CompilerParams(dimension_semantics=("parallel",), vmem_limit_bytes=VMEM_LIMIT),
        name="wkv_post")(y, rkv, k_mod, v_src, o2, params)


def _merge_kernel(za_ref, att_ref, wa_ref, wb_ref, ga_ref, gb_ref, o_ref):
    a = _dot(za_ref[...], wa_ref[...])
    b = _dot(att_ref[...], wb_ref[...])
    o_ref[...] = (jax.nn.sigmoid(ga_ref[...]) * a + jax.nn.sigmoid(gb_ref[...]) * b).astype(BF16)


def branch_merge(za, att, w_a, w_b, p_gate):
    m, kd = za.shape
    n = w_a.shape[1]
    tm = _pick(m, (768, 512, 256, 128))
    tn = _pick(n, (1024, 512, 256, 128))
    nj = n // tn
    return pl.pallas_call(
        _merge_kernel, grid=(m // tm, nj),
        in_specs=[pl.BlockSpec((tm, kd), lambda i, j: (i, 0)), pl.BlockSpec((tm, kd), lambda i, j: (i, 0)),
                  pl.BlockSpec((kd, tn), lambda i, j: (0, j)), pl.BlockSpec((kd, tn), lambda i, j: (0, j)),
                  pl.BlockSpec((tm, tn), lambda i, j: (i, j)), pl.BlockSpec((tm, tn), lambda i, j: (i, nj + j))],
        out_specs=pl.BlockSpec((tm, tn), lambda i, j: (i, j)),
        out_shape=jax.ShapeDtypeStruct((m, n), BF16),
        compiler_params=pltpu.CompilerParams(
            dimension_semantics=("parallel", "parallel"), vmem_limit_bytes=VMEM_LIMIT),
        name="branch_merge")(za, att, w_a, w_b, p_gate, p_gate)


def _proj_res_kernel(h_ref, w_ref, x_ref, g_ref, *rest, nk, emit_x, norm_dtype):
    acc_ref = rest[-1]
    k = pl.program_id(1)

    @pl.when(k == 0)
    def _():
        acc_ref[...] = jnp.zeros_like(acc_ref)

    acc_ref[...] += _dot(h_ref[...], w_ref[...])

    @pl.when(k == nk - 1)
    def _():
        x_new = x_ref[...] + acc_ref[...]
        outs = list(rest[:-1])
        if emit_x:
            outs.pop(0)[...] = x_new
        if norm_dtype is not None:
            outs.pop(0)[...] = _row_rms(x_new, g_ref[...]).astype(norm_dtype)


def proj_residual(h, w, x, gain, *, emit_x, norm_dtype):
    m, kd = h.shape
    d = w.shape[1]
    tm = _pick(m, (384, 256, 128))
    tk = _pick(kd, (2048, 1024, 512))
    nk = kd // tk
    row = pl.BlockSpec((tm, d), lambda i, k: (i, 0))
    out_shape, out_specs = [], []
    if emit_x:
        out_shape.append(jax.ShapeDtypeStruct((m, d), F32))
        out_specs.append(row)
    if norm_dtype is not None:
        out_shape.append(jax.ShapeDtypeStruct((m, d), norm_dtype))
        out_specs.append(row)
    gain = jnp.ones((d,), F32) if gain is None else gain
    outs = pl.pallas_call(
        functools.partial(_proj_res_kernel, nk=nk, emit_x=emit_x, norm_dtype=norm_dtype),
        grid=(m // tm, nk),
        in_specs=[pl.BlockSpec((tm, tk), lambda i, k: (i, k)), pl.BlockSpec((tk, d), lambda i, k: (k, 0)),
                  row, pl.BlockSpec((1, d), lambda i, k: (0, 0))],
        out_specs=out_specs, out_shape=out_shape,
        scratch_shapes=[pltpu.VMEM((tm, d), F32)],
        compiler_params=pltpu.CompilerParams(
            dimension_semantics=("parallel", "arbitrary"), vmem_limit_bytes=VMEM_LIMIT),
        name="proj_residual")(h, w, x, gain.reshape(1, d))
    return outs[0] if len(outs) == 1 else outs


def _ffn_kernel(xf_ref, xp_ref, wu_ref, wg_ref, cw_ref, ov1_ref, ov2_ref, o_ref, *, n_prompt_tiles, dec_seq):
    i = pl.program_id(1)
    xf = xf_ref[...]
    wg = wg_ref[...]
    u = _dot(xf, wu_ref[...])
    g = _dot(xf, wg)
    tm = g.shape[0]
    is_sample = i >= n_prompt_tiles
    gp = _dot(xp_ref[...], wg) * jnp.where(jnp.logical_and(i > 0, jnp.logical_not(is_sample)), 1.0, 0.0)
    row = lax.broadcasted_iota(jnp.int32, (tm, 1), 0)
    g1 = jnp.where(row == 0, gp[7:8], pltpu.roll(g, 1, 0))
    g2 = jnp.where(row == 0, gp[6:7], jnp.where(row == 1, gp[7:8], pltpu.roll(g, 2, 0)))
    s = (row & (dec_seq - 1)) + jnp.where(is_sample, 0, dec_seq)
    g1 = jnp.where(s == 0, ov1_ref[...], g1)
    g2 = jnp.where(s < 2, ov2_ref[...], g2)
    cw = cw_ref[...]
    conv = cw[0:1] * g2 + cw[1:2] * g1 + cw[2:3] * g + cw[3:4]
    o_ref[...] = (conv * jax.nn.sigmoid(conv) * u).astype(BF16)


def conv_ffn_hidden(xf, w_up, w_gate, cw, ov1, ov2, n_prompt, dec_seq):
    m, d = xf.shape
    d_ff = w_up.shape[1]
    tm = ov1.shape[0]
    tn = _pick(d_ff, (1024, 512, 256, 128))
    assert n_prompt % tm == 0 and m == n_prompt + tm and dec_seq >= 2
    wspec = pl.BlockSpec((d, tn), lambda j, i: (0, j))
    cspec = pl.BlockSpec((tm, tn), lambda j, i: (0, j))
    return pl.pallas_call(
        functools.partial(_ffn_kernel, n_prompt_tiles=n_prompt // tm, dec_seq=dec_seq),
        grid=(d_ff // tn, m // tm),
        in_specs=[pl.BlockSpec((tm, d), lambda j, i: (i, 0)),
                  pl.BlockSpec((8, d), lambda j, i: (jnp.maximum(i * (tm // 8) - 1, 0), 0)),
                  wspec, wspec, pl.BlockSpec((8, tn), lambda j, i: (0, j)), cspec, cspec],
        out_specs=pl.BlockSpec((tm, tn), lambda j, i: (i, j)),
        out_shape=jax.ShapeDtypeStruct((m, d_ff), BF16),
        compiler_params=pltpu.CompilerParams(
            dimension_semantics=("parallel", "arbitrary"), vmem_limit_bytes=VMEM_LIMIT),
        name="conv_ffn")(xf, xf, w_up, w_gate, cw, ov1, ov2)


def _rms_norm(x, g):
    return x * lax.rsqrt(jnp.mean(x * x, axis=-1, keepdims=True) + RMS_EPS) * g


def _pair_states(s):
    bsz, n_heads = s.shape[:2]
    st = jnp.swapaxes(s, -1, -2).reshape(bsz, n_heads // 2, 2, HEAD_DIM, HEAD_DIM)
    z = jnp.zeros_like(st[:, :, 0])
    top = jnp.concatenate([st[:, :, 0], z], axis=-1)
    bot = jnp.concatenate([z, st[:, :, 1]], axis=-1)
    return jnp.concatenate([top, bot], axis=-2)


def _unpair_states(sp):
    bsz, n_pairs = sp.shape[:2]
    h0 = sp[:, :, :HEAD_DIM, :HEAD_DIM]
    h1 = sp[:, :, HEAD_DIM:, HEAD_DIM:]
    st = jnp.stack([h0, h1], axis=2).reshape(bsz, 2 * n_pairs, HEAD_DIM, HEAD_DIM)
    return jnp.swapaxes(st, -1, -2)


def kernel(x_prompt, x_sample, cache_sb_k, cache_sb_v, state_shift, state_wkv, state_conv, page_table, norm_mix, w_in, mu_rkv, mu_x, w0, w1, w2, a0, a1, a2, g1, g2, mu_vr, v0, v1, v2, k_k, k_a, r_k, ln_x_w, ln_x_b, w_a, sb_bias, w_b, w_o, norm_ffn, w_up, w_gate, conv_w, conv_b, w_down, norm_final):
    depth = w_in.shape[0]
    bp, seq, d_model = x_prompt.shape
    bd, dec_seq, _ = x_sample.shape
    assert bp == 1, "one prompt sequence is concatenated with the sample rows"
    n_pool = cache_sb_k.shape[1]
    n_heads_a = state_wkv.shape[2]
    d_a = n_heads_a * HEAD_DIM
    n_heads_b = cache_sb_k.shape[3]
    d_b = n_heads_b * HEAD_DIM
    d_ff = w_up.shape[2]
    n_p = bp * seq
    n_s = bd * dec_seq
    n_pages = page_table.shape[1]
    assert n_heads_b * dec_seq == LANES, "sample queries of all heads fill one 128-row tile"

    pool_k = jnp.transpose(cache_sb_k, (0, 1, 3, 4, 2)).reshape(depth * n_pool, d_b, PAGE_SIZE)
    pool_v = jnp.transpose(cache_sb_v, (0, 1, 3, 4, 2)).reshape(depth * n_pool, d_b, PAGE_SIZE)
    pt_flat = page_table.reshape(-1).astype(jnp.int32)

    x = jnp.concatenate([x_prompt.reshape(n_p, d_model), x_sample.reshape(n_s, d_model)], axis=0)
    tm = n_s
    last_rows = jnp.concatenate([jnp.array([n_p - 1]), n_p + dec_seq * jnp.arange(bd) + dec_seq - 1])
    rkv0 = None
    outs = []
    for l in range(depth):
        w_in_b = w_in[l].astype(BF16)
        xcat = norm_shift(x, norm_mix[l], jnp.repeat(state_shift[l], dec_seq, axis=0), n_p, dec_seq)
        xn_last = _rms_norm(x[last_rows], norm_mix[l])

        c0 = 3 * d_a
        w_rkv = w_in[l][:, :c0]
        rkv = matmul(xcat, jnp.concatenate([w_rkv * (1.0 - mu_rkv[l]), w_rkv * mu_rkv[l]], axis=0).astype(BF16))
        q_b = matmul(xcat, (w_in[l][:, c0:c0 + d_b] * (LOG2E * HEAD_DIM ** -0.5)).astype(BF16), (BF16,))
        k_s, k_b = matmul(xcat, w_in_b[:, c0 + d_b:c0 + 2 * d_b], (F32, BF16))
        v_s, v_b = matmul(xcat, w_in_b[:, c0 + 2 * d_b:c0 + 3 * d_b], (F32, BF16))
        p_gate = matmul(xcat, w_in_b[:, c0 + 3 * d_b:])

        firsts = [w1[l], a1[l], g1[l]]
        mus = [mu_x[l, 0], mu_x[l, 1], mu_x[l, 2]]
        seconds = [w2[l], a2[l], g2[l]]
        if l > 0:
            firsts.append(v1[l - 1])
            mus.append(mu_vr[l - 1])
            seconds.append(v2[l - 1])
        widths = [f.shape[1] for f in firsts]
        tot = sum(widths)
        pad = (-tot) % LANES
        top = jnp.concatenate([(1.0 - m)[:, None] * f for m, f in zip(mus, firsts)]
                              + [jnp.zeros((d_model, pad), F32)], axis=1)
        bot = jnp.concatenate([m[:, None] * f for m, f in zip(mus, firsts)]
                              + [jnp.zeros((d_model, pad), F32)], axis=1)
        w_l1 = jnp.concatenate([top, bot], axis=0).astype(BF16)
        offs = [0]
        for wd in widths:
            offs.append(offs[-1] + wd)
        h2 = matmul(xcat, w_l1, (BF16,), act=(offs[1], offs[2], offs[3]))
        w_l2 = jnp.zeros((tot + pad, len(seconds) * d_a), F32)
        for i, s2 in enumerate(seconds):
            w_l2 = w_l2.at[offs[i]:offs[i + 1], i * d_a:(i + 1) * d_a].set(s2)
        o2 = matmul(h2, w_l2.astype(BF16))

        zero_row = jnp.zeros((d_a,), F32)
        prep_params = jnp.stack([w0[l], a0[l], k_k[l], k_a[l], v0[l - 1] if l > 0 else zero_row,
                                 zero_row, zero_row, zero_row])
        prep = wkv_prep(rkv, o2, rkv0 if l > 0 else None, prep_params, d_a, tm)
        if l == 0:
            rkv0 = rkv
            (log_decay, k_mod, a_in, b_in), v_src, v_col = prep, rkv, 2
        else:
            (log_decay, k_mod, a_in, b_in, v_src), v_col = prep, 0
        scan_in = (rkv, log_decay, k_mod, v_src, a_in, b_in)
        cols = (0, 0, 0, v_col, 0, 0)
        s0_p = jnp.zeros((bp, n_heads_a // 2, LANES, LANES), F32)
        y_p, sf_p = wkv7(*scan_in, s0_p, row0=0, n_seq=bp, n_chunks=seq // WKV_CHUNK, c_len=WKV_CHUNK,
                         col_blocks=cols)
        y_s, sf_s = wkv7(*scan_in, _pair_states(state_wkv[l]), row0=n_p, n_seq=bd, n_chunks=1, c_len=dec_seq,
                         col_blocks=cols)
        post_params = jnp.stack([ln_x_w[l], ln_x_b[l], r_k[l].reshape(d_a)] + [zero_row] * 5)
        za = wkv_post(jnp.concatenate([y_p, y_s], axis=0), rkv, k_mod, v_src, v_col, o2, post_params, d_a, tm)

        bias2 = sb_bias[l].astype(F32) * LOG2E
        att_p = sb_prompt(q_b, k_b, v_b, bias2, seq_len=n_p)
        q_smp = q_b[n_p:].reshape(bd, dec_seq, n_heads_b, HEAD_DIM)
        head_eye = jnp.eye(n_heads_b, dtype=BF16)
        qbd = jnp.einsum('bthd,hg->bhtgd', q_smp, head_eye).reshape(bd, n_heads_b * dec_seq, d_b)
        bias_col = jnp.repeat(bias2, dec_seq)[:, None]
        pad_rows = ((0, 0), (0, PAGE_SIZE - dec_seq), (0, 0))
        k_new = jnp.pad(k_b[n_p:].reshape(bd, dec_seq, d_b), pad_rows)
        v_new = jnp.pad(v_b[n_p:].reshape(bd, dec_seq, d_b), pad_rows)
        att_s = sb_sample(qbd, bias_col, k_new, v_new, pool_k, pool_v, pt_flat + l * n_pool,
                          dec_seq=dec_seq, n_heads=n_heads_b)
        att = jnp.concatenate([att_p, att_s.reshape(n_s, d_b)], axis=0)

        merged = branch_merge(za, att, w_a[l].astype(BF16), w_b[l].astype(BF16), p_gate)
        x, xf = proj_residual(merged, w_o[l].astype(BF16), x, norm_ffn[l], emit_x=True, norm_dtype=BF16)

        w_gate_b = w_gate[l].astype(BF16)
        cs = state_conv[l]
        s_idx = (jnp.arange(n_s) % dec_seq)[:, None]
        ov1 = jnp.repeat(cs[:, 1], dec_seq, axis=0)
        ov2 = jnp.where(s_idx == 0, jnp.repeat(cs[:, 0], dec_seq, axis=0), ov1)
        cw = jnp.concatenate([conv_w[l], conv_b[l][None], jnp.zeros((4, d_ff), F32)], axis=0)
        hidden = conv_ffn_hidden(xf, w_up[l].astype(BF16), w_gate_b, cw, ov1, ov2, n_p, dec_seq)
        tail = jnp.concatenate([jnp.array([n_p - 2, n_p - 1]),
                                (last_rows[1:, None] + jnp.array([-1, 0])).reshape(-1)])
        g_tail = matmul(jnp.pad(xf[tail], ((0, LANES - tail.shape[0]), (0, 0))), w_gate_b)
        if l < depth - 1:
            x = proj_residual(hidden, w_down[l].astype(BF16), x, None, emit_x=True, norm_dtype=None)
        else:
            y = proj_residual(hidden, w_down[l].astype(BF16), x, norm_final, emit_x=False, norm_dtype=F32)

        outs.append(dict(
            k_p=k_s[:n_p].reshape(bp, seq, n_heads_b, HEAD_DIM),
            v_p=v_s[:n_p].reshape(bp, seq, n_heads_b, HEAD_DIM),
            shift_p=xn_last[:1],
            wkv_p=_unpair_states(sf_p),
            conv_p=g_tail[:2][None],
            k_s=k_s[n_p:].reshape(bd, dec_seq, n_heads_b, HEAD_DIM),
            v_s=v_s[n_p:].reshape(bd, dec_seq, n_heads_b, HEAD_DIM),
            shift_s=xn_last[1:],
            wkv_s=_unpair_states(sf_s),
            conv_s=g_tail[2:2 + 2 * bd].reshape(bd, 2, d_ff)))

    def stack(name):
        return jnp.stack([o[name] for o in outs])

    return (y[:n_p].reshape(bp, seq, d_model), y[n_p:].reshape(bd, dec_seq, d_model),
            stack('shift_p'), stack('wkv_p'), stack('conv_p'), stack('k_p'), stack('v_p'),
            stack('shift_s'), stack('wkv_s'), stack('conv_s'), stack('k_s'), stack('v_s'))
```

```python
import functools

import jax
import jax.numpy as jnp
from jax import lax
from jax.experimental import pallas as pl
from jax.experimental.pallas import tpu as pltpu

F32 = jnp.float32
BF16 = jnp.bfloat16

HEAD_DIM = 64
LANES = 128
PAGE_SIZE = 128
RMS_EPS = 1e-6
GN_EPS = 64e-5
LOG2E = 1.4426950408889634
WKV_CHUNK = 64
VMEM_LIMIT = 56 * 1024 * 1024

NN = (((1,), (0,)), ((), ()))
NT = (((1,), (1,)), ((), ()))
TN = (((0,), (0,)), ((), ()))


def _dot(a, b, dims=NN):
    return lax.dot_general(a, b, dims, preferred_element_type=F32)


def _split2(x):
    hi = x.astype(BF16)
    lo = (x - hi.astype(F32)).astype(BF16)
    return hi, lo


def _dot3(a, b, dims=NN):
    ah, al = _split2(a)
    bh, bl = _split2(b)
    return _dot(ah, bh, dims) + (_dot(ah, bl, dims) + _dot(al, bh, dims))


def _pick(dim, candidates):
    for c in candidates:
        if dim % c == 0:
            return c
    return dim


def _lora_act(h, bounds):
    col = lax.broadcasted_iota(jnp.int32, h.shape, 1)
    is_tanh = col < bounds[0]
    is_sig = jnp.logical_and(col >= bounds[1], col < bounds[2])
    return jnp.where(is_tanh, jnp.tanh(h), jnp.where(is_sig, jax.nn.sigmoid(h), h))


def _mm_kernel(x_ref, w_ref, *rest, nk, act):
    o_refs = rest[:-1] if nk > 1 else rest
    k = pl.program_id(2)

    def finish(acc):
        if act is not None:
            acc = _lora_act(acc, act)
        for o_ref in o_refs:
            o_ref[...] = acc.astype(o_ref.dtype)

    if nk == 1:
        finish(_dot(x_ref[...], w_ref[...]))
        return
    acc_ref = rest[-1]

    @pl.when(k == 0)
    def _():
        acc_ref[...] = jnp.zeros_like(acc_ref)

    acc_ref[...] += _dot(x_ref[...], w_ref[...])

    @pl.when(k == nk - 1)
    def _():
        finish(acc_ref[...])


def matmul(x, w, out_dtypes=(F32,), act=None):
    m = x.shape[0]
    kd, n = w.shape
    tm = _pick(m, (768, 512, 256, 128, 64, 32, 16, 8))
    tn = _pick(n, (1024, 512, 256, 128))
    tk = _pick(kd, (2048, 1024, 512, 256, 128))
    nk = kd // tk
    outs = pl.pallas_call(
        functools.partial(_mm_kernel, nk=nk, act=act),
        grid=(m // tm, n // tn, nk),
        in_specs=[pl.BlockSpec((tm, tk), lambda i, j, k: (i, k)),
                  pl.BlockSpec((tk, tn), lambda i, j, k: (k, j))],
        out_specs=[pl.BlockSpec((tm, tn), lambda i, j, k: (i, j))] * len(out_dtypes),
        out_shape=[jax.ShapeDtypeStruct((m, n), dt) for dt in out_dtypes],
        scratch_shapes=[pltpu.VMEM((tm, tn), F32)] if nk > 1 else [],
        compiler_params=pltpu.CompilerParams(
            dimension_semantics=("parallel", "parallel", "arbitrary"), vmem_limit_bytes=VMEM_LIMIT),
        name="mm")(x, w)
    return outs[0] if len(out_dtypes) == 1 else outs


def _bf(x):
    return x.astype(BF16)


def _wkv_chunk_terms(rs, lws, ks, vs, as_, bs, c_len):
    n = len(rs)
    c2 = 2 * c_len
    row = lax.broadcasted_iota(jnp.int32, (c_len, c_len), 0)
    col = lax.broadcasted_iota(jnp.int32, (c_len, c_len), 1)
    eye = (col == row).astype(F32)
    tri = (col <= row).astype(BF16)
    r2 = lax.broadcasted_iota(jnp.int32, (c2, c2), 0)
    j2 = lax.broadcasted_iota(jnp.int32, (c2, c2), 1) & (c_len - 1)
    t2 = r2 & (c_len - 1)
    keep = jnp.logical_or(j2 < t2, jnp.logical_and(r2 >= c_len, j2 == t2))
    lane = lax.broadcasted_iota(jnp.int32, (c_len, LANES), 1)
    m0 = lane < HEAD_DIM
    masks = (m0, jnp.logical_not(m0))
    rr = lax.broadcasted_iota(jnp.int32, (LANES, LANES), 0)
    cc = lax.broadcasted_iota(jnp.int32, (LANES, LANES), 1)
    same_head = (rr < HEAD_DIM) == (cc < HEAD_DIM)
    diag = rr == cc
    n_double = max(c_len.bit_length() - 2, 0)
    zeros = jnp.zeros((c_len, LANES), F32)

    ats, rts, qs, bks, wcs, vpads = [], [], [], [], [], []
    for p in range(n):
        lw = lws[p]
        l1 = _bf(lw)
        rem = lw - l1.astype(F32)
        l2 = _bf(rem)
        l3 = _bf(rem - l2.astype(F32))
        lam = _dot(tri, l1) + (_dot(tri, l2) + _dot(tri, l3))
        lam_c = lam[c_len - 1:c_len, :]
        e_neg = jnp.exp(-lam)
        e_end = jnp.exp(lam_c - lam)
        ats.append(as_[p] * jnp.exp(lam - lw))
        rts.append(rs[p] * jnp.exp(lam))
        qs.append(_bf(jnp.concatenate([bs[p] * e_neg, ks[p] * e_neg], axis=0)))
        bks.append(_bf(jnp.concatenate([bs[p] * e_end, ks[p] * e_end], axis=0)))
        wcs.append(jnp.exp(lam_c))
        vpads.append(_bf(jnp.concatenate([zeros, vs[p]], axis=0)))

    units = [(p, h) for p in range(n) for h in range(2)]
    ms = []
    for p, h in units:
        lhs = jnp.concatenate([jnp.where(masks[h], ats[p], 0.0), jnp.where(masks[h], rts[p], 0.0)], axis=0)
        ms.append(jnp.where(keep, _dot(_bf(lhs), qs[p], NT), 0.0))
    tops = [_bf(m[:c_len]) for m in ms]
    bots = [_bf(m[c_len:]) for m in ms]
    pws = [m[:c_len, :c_len] for m in ms]
    tinvs = [eye + x for x in pws]
    for _ in range(n_double):
        pws = [_dot(_bf(x), _bf(x)) for x in pws]
        tinvs = [t + _dot(_bf(t), _bf(x)) for t, x in zip(tinvs, pws)]
    tinvs = [_bf(t) for t in tinvs]
    at_b = [_bf(x) for x in ats]
    tas = [_dot(t, at_b[p]) for t, (p, h) in zip(tinvs, units)]
    lakv = [_dot(top, vpads[p]) for top, (p, h) in zip(tops, units)]
    u0s = [_dot(t, _bf(x)) for t, x in zip(tinvs, lakv)]
    ta_p = [_bf(jnp.where(m0, tas[2 * p], tas[2 * p + 1])) for p in range(n)]
    u0_p = [jnp.where(m0, u0s[2 * p], u0s[2 * p + 1]) for p in range(n)]
    uv_p = [_bf(jnp.concatenate([u0_p[p], vs[p]], axis=0)) for p in range(n)]
    rcs = [_dot(bot[:, :c_len], ta_p[p]) for bot, (p, h) in zip(bots, units)]
    y0s = [_dot(bot, uv_p[p]) for bot, (p, h) in zip(bots, units)]
    rc = [rts[p] + jnp.where(m0, rcs[2 * p], rcs[2 * p + 1]) for p in range(n)]
    y0 = [jnp.where(m0, y0s[2 * p], y0s[2 * p + 1]) for p in range(n)]
    pc = [jnp.where(same_head, _dot(bks[p][:c_len], ta_p[p], TN), 0.0)
          + jnp.where(diag, jnp.broadcast_to(wcs[p], (LANES, LANES)), 0.0) for p in range(n)]
    d0 = [jnp.where(same_head, _dot(bks[p], uv_p[p], TN), 0.0) for p in range(n)]
    return rc, y0, pc, d0


def _wkv_terms_kernel(r_ref, lw_ref, k_ref, v_ref, a_ref, b_ref, rc_ref, y0_ref, pc_ref, d0_ref,
                      *, c_len, pairs):
    sls = [slice(p * LANES, (p + 1) * LANES) for p in range(pairs)]
    rc, y0, pc, d0 = _wkv_chunk_terms(*[[ref[:, sl] for sl in sls]
                                        for ref in (r_ref, lw_ref, k_ref, v_ref, a_ref, b_ref)], c_len)
    for p, sl in enumerate(sls):
        rc_ref[:, sl] = rc[p]
        y0_ref[:, sl] = y0[p]
        pc_ref[0, p] = pc[p]
        d0_ref[0, p] = d0[p]


def _wkv_sweep_kernel(rc_ref, y0_ref, pc_ref, d0_ref, s0_ref, y_ref, sout_ref, st_ref, *, n_chunks, n_pairs):
    c = pl.program_id(1)

    @pl.when(c == 0)
    def _():
        st_ref[...] = s0_ref[0]

    for p in range(n_pairs):
        sl = slice(p * LANES, (p + 1) * LANES)
        st = st_ref[p]
        y_ref[:, sl] = _dot3(rc_ref[:, sl], st) + y0_ref[:, sl]
        st_ref[p] = _dot3(pc_ref[0, p], st) + d0_ref[0, p]

    @pl.when(c == n_chunks - 1)
    def _():
        sout_ref[0] = st_ref[...]


def wkv7(r, lw, k, v, a, b, s0, *, row0, n_seq, n_chunks, c_len, col_blocks=(0, 0, 0, 0, 0, 0)):
    width = lw.shape[1]
    n_pairs = width // LANES
    pairs = n_pairs
    tot_chunks = n_seq * n_chunks
    rows = tot_chunks * c_len
    blk0 = row0 // c_len
    lw_cols = pairs * LANES
    in_specs = [pl.BlockSpec((c_len, lw_cols), functools.partial(lambda c, g, cb: (blk0 + c, cb), cb=cb))
                for cb in col_blocks]
    row_spec = pl.BlockSpec((c_len, lw_cols), lambda c, g: (c, g))
    mat_spec = pl.BlockSpec((1, pairs, LANES, LANES), lambda c, g: (c, g, 0, 0))
    rc, y0, pc, d0 = pl.pallas_call(
        functools.partial(_wkv_terms_kernel, c_len=c_len, pairs=pairs),
        grid=(tot_chunks, n_pairs // pairs),
        in_specs=in_specs,
        out_specs=[row_spec, row_spec, mat_spec, mat_spec],
        out_shape=[jax.ShapeDtypeStruct((rows, width), F32), jax.ShapeDtypeStruct((rows, width), F32),
                   jax.ShapeDtypeStruct((tot_chunks, n_pairs, LANES, LANES), F32),
                   jax.ShapeDtypeStruct((tot_chunks, n_pairs, LANES, LANES), F32)],
        compiler_params=pltpu.CompilerParams(
            dimension_semantics=("parallel", "parallel"), vmem_limit_bytes=VMEM_LIMIT),
        name="wkv_terms")(r, lw, k, v, a, b)

    rowp = pl.BlockSpec((c_len, width), lambda s, c: (s * n_chunks + c, 0))
    matp = pl.BlockSpec((1, n_pairs, LANES, LANES), lambda s, c: (s * n_chunks + c, 0, 0, 0))
    stp = pl.BlockSpec((1, n_pairs, LANES, LANES), lambda s, c: (s, 0, 0, 0))
    y, s_fin = pl.pallas_call(
        functools.partial(_wkv_sweep_kernel, n_chunks=n_chunks, n_pairs=n_pairs),
        grid=(n_seq, n_chunks),
        in_specs=[rowp, rowp, matp, matp, stp],
        out_specs=[rowp, stp],
        out_shape=[jax.ShapeDtypeStruct((rows, width), F32),
                   jax.ShapeDtypeStruct((n_seq, n_pairs, LANES, LANES), F32)],
        scratch_shapes=[pltpu.VMEM((n_pairs, LANES, LANES), F32)],
        compiler_params=pltpu.CompilerParams(
            dimension_semantics=("parallel", "arbitrary"), vmem_limit_bytes=VMEM_LIMIT),
        name="wkv_sweep")(rc, y0, pc, d0, s0)
    return y, s_fin


def _tri_ones():
    jr = lax.broadcasted_iota(jnp.int32, (LANES, 2 * LANES), 0)
    jc = lax.broadcasted_iota(jnp.int32, (LANES, 2 * LANES), 1)
    return jnp.logical_or(jc >= LANES, jr >= jc).astype(BF16)


def _sb_sweep(zl, valid, carry, tri_ones):
    n_sub = zl.shape[1] // LANES
    sls = [slice(s * LANES, (s + 1) * LANES) for s in range(n_sub)]
    sps = []
    for sl in sls:
        z_s = zl[:, sl]
        sp = jnp.maximum(z_s, 0.0) + jnp.log(1.0 + jnp.exp2(-jnp.abs(z_s))) * LOG2E
        if valid is not None:
            sp = jnp.where(valid[:, sl], sp, 0.0)
        sps.append(sp.astype(BF16))
    exts = [_dot(sp, tri_ones) for sp in sps]
    atts = [None] * n_sub
    for s in reversed(range(n_sub)):
        att = jnp.exp2(zl[:, sls[s]] - (exts[s][:, :LANES] + carry))
        if valid is not None:
            att = jnp.where(valid[:, sls[s]], att, 0.0)
        atts[s] = att.astype(BF16)
        carry = carry + exts[s][:, LANES:]
    return (jnp.concatenate(atts, axis=1) if n_sub > 1 else atts[0]), carry


def _sb_prompt_kernel(q_ref, bias_ref, k_ref, v_ref, o_ref, *, tq, kb):
    i = pl.program_id(1)
    q = q_ref[...]
    lane = lax.broadcasted_iota(jnp.int32, (tq, LANES), 1)
    m0 = lane < HEAD_DIM
    zero = jnp.zeros_like(q)
    qs = jnp.concatenate([jnp.where(m0, q, zero), jnp.where(m0, zero, q)], axis=0)
    bias = bias_ref[0]
    tri_ones = _tri_ones()

    def logits(j):
        return _dot(qs, k_ref[pl.ds(pl.multiple_of(j * kb, kb), kb), :], NT) + bias

    def weighted(att, j):
        return _dot(att, v_ref[pl.ds(pl.multiple_of(j * kb, kb), kb), :])

    jd = (i * tq) // kb
    row = lax.broadcasted_iota(jnp.int32, (2 * tq, kb), 0)
    col = lax.broadcasted_iota(jnp.int32, (2 * tq, kb), 1)
    valid = (jd * kb + col) < i * tq + jnp.where(row >= tq, row - tq, row)
    att, carry = _sb_sweep(logits(jd), valid, jnp.zeros((2 * tq, LANES), F32), tri_ones)

    def body(jj, state):
        acc, carry, zl, att_prev = state
        j = jd - 1 - jj
        acc = acc + weighted(att_prev, j + 1)
        zl_next = logits(jnp.maximum(j - 1, 0))
        att, carry = _sb_sweep(zl, None, carry, tri_ones)
        return acc, carry, zl_next, att

    acc, carry, _, att = lax.fori_loop(
        0, jd, body, (jnp.zeros((2 * tq, LANES), F32), carry, logits(jnp.maximum(jd - 1, 0)), att))
    acc = acc + weighted(att, 0)
    o_ref[...] = jnp.where(m0, acc[:tq], acc[tq:]).astype(o_ref.dtype)


def sb_prompt(q, k, v, bias, *, seq_len, tq=128, kb=512):
    width = q.shape[1]
    n_pairs = width // LANES
    nq = seq_len // tq
    assert kb % tq == 0 and seq_len % kb == 0
    bias_cols = jnp.repeat(bias.astype(F32).reshape(n_pairs, 2), tq, axis=1).reshape(n_pairs, 2 * tq, 1)
    return pl.pallas_call(
        functools.partial(_sb_prompt_kernel, tq=tq, kb=kb),
        grid=(n_pairs, nq),
        in_specs=[pl.BlockSpec((tq, LANES), lambda p, i: (i, p)),
                  pl.BlockSpec((1, 2 * tq, 1), lambda p, i: (p, 0, 0)),
                  pl.BlockSpec((seq_len, LANES), lambda p, i: (0, p)),
                  pl.BlockSpec((seq_len, LANES), lambda p, i: (0, p))],
        out_specs=pl.BlockSpec((tq, LANES), lambda p, i: (i, p)),
        out_shape=jax.ShapeDtypeStruct((seq_len, width), BF16),
        compiler_params=pltpu.CompilerParams(
            dimension_semantics=("parallel", "arbitrary"), vmem_limit_bytes=VMEM_LIMIT),
        name="sb_prompt")(q, bias_cols, k, v)


def _sb_sample_kernel(pt_ref, qbd_ref, bias_ref, kn_ref, vn_ref, *rest, n_steps, dec_seq, n_heads, n_pg):
    kp_refs, vp_refs = rest[:n_pg], rest[n_pg:2 * n_pg]
    o_ref, acc_ref, carry_ref = rest[2 * n_pg:]
    j = pl.program_id(1)
    rows = qbd_ref.shape[1]
    tri_ones = _tri_ones()
    qbd = qbd_ref[0]
    bias = bias_ref[...]

    @pl.when(j == 0)
    def _():
        row = lax.broadcasted_iota(jnp.int32, (rows, PAGE_SIZE), 0)
        col = lax.broadcasted_iota(jnp.int32, (rows, PAGE_SIZE), 1)
        zl = _dot(qbd, kn_ref[0], NT) + bias
        att, carry = _sb_sweep(zl, col < row % dec_seq, jnp.zeros((rows, LANES), F32), tri_ones)
        acc_ref[...] = _dot(att, vn_ref[0])
        carry_ref[...] = carry

    @pl.when(j > 0)
    def _():
        kt = jnp.concatenate([kp_refs[i][0].astype(BF16) for i in range(n_pg)], axis=1)
        vt = jnp.concatenate([vp_refs[i][0].astype(BF16) for i in range(n_pg)], axis=1)
        att, carry = _sb_sweep(_dot(qbd, kt) + bias, None, carry_ref[...], tri_ones)
        acc_ref[...] += _dot(att, vt, NT)
        carry_ref[...] = carry

    @pl.when(j == n_steps - 1)
    def _():
        lane = lax.broadcasted_iota(jnp.int32, (dec_seq, n_heads * HEAD_DIM), 1)
        out = jnp.zeros((dec_seq, n_heads * HEAD_DIM), F32)
        for h in range(n_heads):
            blk = acc_ref[h * dec_seq:(h + 1) * dec_seq, :]
            out = jnp.where(lane // HEAD_DIM == h, blk, out)
        o_ref[0] = out.astype(o_ref.dtype)


def sb_sample(qbd, bias_col, k_new, v_new, pool_kt, pool_vt, page_ids, *, dec_seq, n_heads, n_pg=8):
    bsz, rows, width = qbd.shape
    n_pages = page_ids.shape[0] // bsz
    assert n_pages % n_pg == 0
    n_steps = n_pages // n_pg + 1

    def pool_idx(i):
        return lambda b, j, pt: (pt[b * n_pages + n_pages - n_pg * jnp.maximum(j, 1) + i], 0, 0)

    page_specs = [pl.BlockSpec((1, width, PAGE_SIZE), pool_idx(i)) for i in range(n_pg)]
    grid_spec = pltpu.PrefetchScalarGridSpec(
        num_scalar_prefetch=1, grid=(bsz, n_steps),
        in_specs=[pl.BlockSpec((1, rows, width), lambda b, j, pt: (b, 0, 0)),
                  pl.BlockSpec((rows, 1), lambda b, j, pt: (0, 0)),
                  pl.BlockSpec((1, PAGE_SIZE, width), lambda b, j, pt: (b, 0, 0)),
                  pl.BlockSpec((1, PAGE_SIZE, width), lambda b, j, pt: (b, 0, 0))] + page_specs + page_specs,
        out_specs=pl.BlockSpec((1, dec_seq, width), lambda b, j, pt: (b, 0, 0)),
        scratch_shapes=[pltpu.VMEM((rows, width), F32), pltpu.VMEM((rows, PAGE_SIZE), F32)])
    return pl.pallas_call(
        functools.partial(_sb_sample_kernel, n_steps=n_steps, dec_seq=dec_seq, n_heads=n_heads, n_pg=n_pg),
        grid_spec=grid_spec,
        out_shape=jax.ShapeDtypeStruct((bsz, dec_seq, width), BF16),
        compiler_params=pltpu.CompilerParams(
            dimension_semantics=("parallel", "arbitrary"), vmem_limit_bytes=VMEM_LIMIT),
        name="sb_sample")(page_ids, qbd, bias_col, k_new, v_new, *([pool_kt] * n_pg), *([pool_vt] * n_pg))


def _row_rms(x, g):
    return x * lax.rsqrt(jnp.mean(x * x, axis=-1, keepdims=True) + RMS_EPS) * g


def _norm_shift_kernel(x_ref, xp_ref, g_ref, first_ref, o_ref, *, n_prompt_tiles, dec_seq):
    i = pl.program_id(0)
    g = g_ref[...]
    xn = _row_rms(x_ref[...], g)
    tm, d = xn.shape
    above = _row_rms(xp_ref[...], g)[7:8]
    is_sample = i >= n_prompt_tiles
    first = jnp.where(is_sample, first_ref[...],
                      jnp.broadcast_to(above, xn.shape) * jnp.where(i > 0, 1.0, 0.0))
    row = lax.broadcasted_iota(jnp.int32, (tm, 1), 0)
    starts = (row & jnp.where(is_sample, dec_seq - 1, -1)) == 0
    prev = jnp.where(starts, first, pltpu.roll(xn, 1, 0))
    o_ref[:, :d] = xn.astype(BF16)
    o_ref[:, d:] = prev.astype(BF16)


def norm_shift(x, gain, first_rows, n_prompt, dec_seq):
    m, d = x.shape
    tm = first_rows.shape[0]
    assert n_prompt % tm == 0 and m == n_prompt + tm and dec_seq & (dec_seq - 1) == 0
    return pl.pallas_call(
        functools.partial(_norm_shift_kernel, n_prompt_tiles=n_prompt // tm, dec_seq=dec_seq),
        grid=(m // tm,),
        in_specs=[pl.BlockSpec((tm, d), lambda i: (i, 0)),
                  pl.BlockSpec((8, d), lambda i: (jnp.maximum(i * (tm // 8) - 1, 0), 0)),
                  pl.BlockSpec((1, d), lambda i: (0, 0)),
                  pl.BlockSpec((tm, d), lambda i: (0, 0))],
        out_specs=pl.BlockSpec((tm, 2 * d), lambda i: (i, 0)),
        out_shape=jax.ShapeDtypeStruct((m, 2 * d), BF16),
        compiler_params=pltpu.CompilerParams(dimension_semantics=("parallel",), vmem_limit_bytes=VMEM_LIMIT),
        name="norm_shift")(x, x, gain.reshape(1, d), first_rows)


def _seg_ones():
    r = lax.broadcasted_iota(jnp.int32, (2 * LANES, LANES), 0) & (LANES - 1)
    c = lax.broadcasted_iota(jnp.int32, (2 * LANES, LANES), 1)
    return ((r < HEAD_DIM) == (c < HEAD_DIM)).astype(BF16)


def _head_sums(x, seg):
    parts = [_dot(jnp.concatenate(_split2(x[:, t:t + LANES]), axis=1), seg)
             for t in range(0, x.shape[1], LANES)]
    return jnp.concatenate(parts, axis=1)


def _wkv_prep_kernel(rkv_ref, o2_ref, *rest, d_a, has_v):
    if has_v:
        vf_ref, p_ref, lw_ref, k_ref, a_ref, b_ref, v_ref = rest
    else:
        p_ref, lw_ref, k_ref, a_ref, b_ref = rest
    p = p_ref[...]
    seg = _seg_ones()
    k = rkv_ref[:, d_a:2 * d_a]
    dw = p[0:1] + o2_ref[:, :d_a]
    w_log = -(jnp.maximum(-dw, 0.0) + jnp.log(1.0 + jnp.exp(-jnp.abs(dw)))) - 0.5
    lw_ref[...] = -jnp.exp(w_log)
    a_rate = jax.nn.sigmoid(p[1:2] + o2_ref[:, d_a:2 * d_a])
    kk = k * p[2:3]
    kk = kk * lax.rsqrt(_head_sums(kk * kk, seg) + 1e-12)
    k_ref[...] = k * (1.0 + (a_rate - 1.0) * p[3:4])
    a_ref[...] = -kk
    b_ref[...] = kk * a_rate
    if has_v:
        v = rkv_ref[:, 2 * d_a:]
        v_ref[...] = v + (vf_ref[...] - v) * jax.nn.sigmoid(p[4:5] + o2_ref[:, 3 * d_a:])


def wkv_prep(rkv, o2, v_first_src, params, d_a, tm):
    m = rkv.shape[0]
    has_v = v_first_src is not None
    row = pl.BlockSpec((tm, d_a), lambda i: (i, 0))
    in_specs = [pl.BlockSpec((tm, rkv.shape[1]), lambda i: (i, 0)),
                pl.BlockSpec((tm, o2.shape[1]), lambda i: (i, 0))]
    args = [rkv, o2]
    if has_v:
        in_specs.append(pl.BlockSpec((tm, d_a), lambda i: (i, 2)))
        args.append(v_first_src)
    in_specs.append(pl.BlockSpec(params.shape, lambda i: (0, 0)))
    n_out = 5 if has_v else 4
    return pl.pallas_call(
        functools.partial(_wkv_prep_kernel, d_a=d_a, has_v=has_v),
        grid=(m // tm,), in_specs=in_specs, out_specs=[row] * n_out,
        out_shape=[jax.ShapeDtypeStruct((m, d_a), F32)] * n_out,
        compiler_params=pltpu.CompilerParams(dimension_semantics=("parallel",), vmem_limit_bytes=VMEM_LIMIT),
        name="wkv_prep")(*args, params)


def _wkv_post_kernel(y_ref, r_ref, k_ref, v_ref, g_ref, p_ref, o_ref):
    p = p_ref[...]
    seg = _seg_ones()
    y = y_ref[...]
    inv = 1.0 / HEAD_DIM
    d = y - _head_sums(y, seg) * inv
    var = _head_sums(d * d, seg) * inv
    y_n = d * lax.rsqrt(var + GN_EPS) * p[0:1] + p[1:2]
    bonus = _head_sums(r_ref[...] * k_ref[...] * p[2:3], seg) * v_ref[...]
    o_ref[...] = ((y_n + bonus) * g_ref[...]).astype(BF16)


def wkv_post(y, rkv, k_mod, v_src, v_col, o2, params, d_a, tm):
    m = y.shape[0]

    def col(c):
        return pl.BlockSpec((tm, d_a), lambda i: (i, c))

    return pl.pallas_call(
        _wkv_post_kernel, grid=(m // tm,),
        in_specs=[col(0), col(0), col(0), col(v_col), col(2), pl.BlockSpec(params.shape, lambda i: (0, 0))],
        out_specs=col(0), out_shape=jax.ShapeDtypeStruct((m, d_a), BF16),
        compiler_params=pltpu.CompilerParams(dimension_semantics=("parallel",), vmem_limit_bytes=VMEM_LIMIT),
        name="wkv_post")(y, rkv, k_mod, v_src, o2, params)


def _merge_kernel(za_ref, att_ref, wa_ref, wb_ref, ga_ref, gb_ref, o_ref):
    a = _dot(za_ref[...], wa_ref[...])
    b = _dot(att_ref[...], wb_ref[...])
    o_ref[...] = (jax.nn.sigmoid(ga_ref[...]) * a + jax.nn.sigmoid(gb_ref[...]) * b).astype(BF16)


def branch_merge(za, att, w_a, w_b, p_gate):
    m, kd = za.shape
    n = w_a.shape[1]
    tm = _pick(m, (768, 512, 256, 128))
    tn = _pick(n, (1024, 512, 256, 128))
    nj = n // tn
    return pl.pallas_call(
        _merge_kernel, grid=(m // tm, nj),
        in_specs=[pl.BlockSpec((tm, kd), lambda i, j: (i, 0)), pl.BlockSpec((tm, kd), lambda i, j: (i, 0)),
                  pl.BlockSpec((kd, tn), lambda i, j: (0, j)), pl.BlockSpec((kd, tn), lambda i, j: (0, j)),
                  pl.BlockSpec((tm, tn), lambda i, j: (i, j)), pl.BlockSpec((tm, tn), lambda i, j: (i, nj + j))],
        out_specs=pl.BlockSpec((tm, tn), lambda i, j: (i, j)),
        out_shape=jax.ShapeDtypeStruct((m, n), BF16),
        compiler_params=pltpu.CompilerParams(
            dimension_semantics=("parallel", "parallel"), vmem_limit_bytes=VMEM_LIMIT),
        name="branch_merge")(za, att, w_a, w_b, p_gate, p_gate)


def _proj_res_kernel(h_ref, w_ref, x_ref, g_ref, *rest, nk, emit_x, norm_dtype):
    acc_ref = rest[-1]
    k = pl.program_id(1)

    @pl.when(k == 0)
    def _():
        acc_ref[...] = jnp.zeros_like(acc_ref)

    acc_ref[...] += _dot(h_ref[...], w_ref[...])

    @pl.when(k == nk - 1)
    def _():
        x_new = x_ref[...] + acc_ref[...]
        outs = list(rest[:-1])
        if emit_x:
            outs.pop(0)[...] = x_new
        if norm_dtype is not None:
            outs.pop(0)[...] = _row_rms(x_new, g_ref[...]).astype(norm_dtype)


def proj_residual(h, w, x, gain, *, emit_x, norm_dtype):
    m, kd = h.shape
    d = w.shape[1]
    tm = _pick(m, (384, 256, 128))
    tk = _pick(kd, (2048, 1024, 512))
    nk = kd // tk
    row = pl.BlockSpec((tm, d), lambda i, k: (i, 0))
    out_shape, out_specs = [], []
    if emit_x:
        out_shape.append(jax.ShapeDtypeStruct((m, d), F32))
        out_specs.append(row)
    if norm_dtype is not None:
        out_shape.append(jax.ShapeDtypeStruct((m, d), norm_dtype))
        out_specs.append(row)
    gain = jnp.ones((d,), F32) if gain is None else gain
    outs = pl.pallas_call(
        functools.partial(_proj_res_kernel, nk=nk, emit_x=emit_x, norm_dtype=norm_dtype),
        grid=(m // tm, nk),
        in_specs=[pl.BlockSpec((tm, tk), lambda i, k: (i, k)), pl.BlockSpec((tk, d), lambda i, k: (k, 0)),
                  row, pl.BlockSpec((1, d), lambda i, k: (0, 0))],
        out_specs=out_specs, out_shape=out_shape,
        scratch_shapes=[pltpu.VMEM((tm, d), F32)],
        compiler_params=pltpu.CompilerParams(
            dimension_semantics=("parallel", "arbitrary"), vmem_limit_bytes=VMEM_LIMIT),
        name="proj_residual")(h, w, x, gain.reshape(1, d))
    return outs[0] if len(outs) == 1 else outs


def _ffn_kernel(xf_ref, xp_ref, wu_ref, wg_ref, cw_ref, ov1_ref, ov2_ref, o_ref, *, n_tiles, sample_off, dec_seq):
    i = pl.program_id(1)
    xf = xf_ref[...]
    wg = wg_ref[...]
    u = _dot(xf, wu_ref[...])
    g = _dot(xf, wg)
    tm = g.shape[0]
    gp = _dot(xp_ref[...], wg) * jnp.where(i > 0, 1.0, 0.0)
    row = lax.broadcasted_iota(jnp.int32, (tm, 1), 0)
    g1 = jnp.where(row == 0, gp[7:8], pltpu.roll(g, 1, 0))
    g2 = jnp.where(row == 0, gp[6:7], jnp.where(row == 1, gp[7:8], pltpu.roll(g, 2, 0)))
    first_sample = jnp.where(i == n_tiles - 1, sample_off, tm)
    s = (row & (dec_seq - 1)) + jnp.where(row >= first_sample, 0, dec_seq)
    g1 = jnp.where(s == 0, ov1_ref[...], g1)
    g2 = jnp.where(s < 2, ov2_ref[...], g2)
    cw = cw_ref[...]
    conv = cw[0:1] * g2 + cw[1:2] * g1 + cw[2:3] * g + cw[3:4]
    o_ref[...] = (conv * jax.nn.sigmoid(conv) * u).astype(BF16)


def conv_ffn_hidden(xf, w_up, w_gate, cw, ov1, ov2, n_prompt, dec_seq):
    m, d = xf.shape
    d_ff = w_up.shape[1]
    n_s = ov1.shape[0]
    tm = next(t for t in (768, 512, 256, n_s) if m % t == 0 and (n_prompt % t) + n_s == t)
    off = n_prompt % tm
    tn = _pick(d_ff, (1024, 512, 256, 128))
    assert off % dec_seq == 0 and off % 8 == 0 and dec_seq >= 2
    ov1 = jnp.pad(ov1, ((off, 0), (0, 0)))
    ov2 = jnp.pad(ov2, ((off, 0), (0, 0)))
    wspec = pl.BlockSpec((d, tn), lambda j, i: (0, j))
    cspec = pl.BlockSpec((tm, tn), lambda j, i: (0, j))
    return pl.pallas_call(
        functools.partial(_ffn_kernel, n_tiles=m // tm, sample_off=off, dec_seq=dec_seq),
        grid=(d_ff // tn, m // tm),
        in_specs=[pl.BlockSpec((tm, d), lambda j, i: (i, 0)),
                  pl.BlockSpec((8, d), lambda j, i: (jnp.maximum(i * (tm // 8) - 1, 0), 0)),
                  wspec, wspec, pl.BlockSpec((8, tn), lambda j, i: (0, j)), cspec, cspec],
        out_specs=pl.BlockSpec((tm, tn), lambda j, i: (i, j)),
        out_shape=jax.ShapeDtypeStruct((m, d_ff), BF16),
        compiler_params=pltpu.CompilerParams(
            dimension_semantics=("parallel", "arbitrary"), vmem_limit_bytes=VMEM_LIMIT),
        name="conv_ffn")(xf, xf, w_up, w_gate, cw, ov1, ov2)


def _rms_norm(x, g):
    return x * lax.rsqrt(jnp.mean(x * x, axis=-1, keepdims=True) + RMS_EPS) * g


def _pair_states(s):
    bsz, n_heads = s.shape[:2]
    st = jnp.swapaxes(s, -1, -2).reshape(bsz, n_heads // 2, 2, HEAD_DIM, HEAD_DIM)
    z = jnp.zeros_like(st[:, :, 0])
    top = jnp.concatenate([st[:, :, 0], z], axis=-1)
    bot = jnp.concatenate([z, st[:, :, 1]], axis=-1)
    return jnp.concatenate([top, bot], axis=-2)


def _unpair_states(sp):
    bsz, n_pairs = sp.shape[:2]
    h0 = sp[:, :, :HEAD_DIM, :HEAD_DIM]
    h1 = sp[:, :, HEAD_DIM:, HEAD_DIM:]
    st = jnp.stack([h0, h1], axis=2).reshape(bsz, 2 * n_pairs, HEAD_DIM, HEAD_DIM)
    return jnp.swapaxes(st, -1, -2)


def kernel(x_prompt, x_sample, cache_sb_k, cache_sb_v, state_shift, state_wkv, state_conv, page_table, norm_mix, w_in, mu_rkv, mu_x, w0, w1, w2, a0, a1, a2, g1, g2, mu_vr, v0, v1, v2, k_k, k_a, r_k, ln_x_w, ln_x_b, w_a, sb_bias, w_b, w_o, norm_ffn, w_up, w_gate, conv_w, conv_b, w_down, norm_final):
    depth = w_in.shape[0]
    bp, seq, d_model = x_prompt.shape
    bd, dec_seq, _ = x_sample.shape
    assert bp == 1, "one prompt sequence is concatenated with the sample rows"
    n_pool = cache_sb_k.shape[1]
    n_heads_a = state_wkv.shape[2]
    d_a = n_heads_a * HEAD_DIM
    n_heads_b = cache_sb_k.shape[3]
    d_b = n_heads_b * HEAD_DIM
    d_ff = w_up.shape[2]
    n_p = bp * seq
    n_s = bd * dec_seq
    n_pages = page_table.shape[1]
    assert n_heads_b * dec_seq == LANES, "sample queries of all heads fill one 128-row tile"

    pool_k = jnp.transpose(cache_sb_k, (0, 1, 3, 4, 2)).reshape(depth * n_pool, d_b, PAGE_SIZE)
    pool_v = jnp.transpose(cache_sb_v, (0, 1, 3, 4, 2)).reshape(depth * n_pool, d_b, PAGE_SIZE)
    pt_flat = page_table.reshape(-1).astype(jnp.int32)

    x = jnp.concatenate([x_prompt.reshape(n_p, d_model), x_sample.reshape(n_s, d_model)], axis=0)
    tm = n_s
    last_rows = jnp.concatenate([jnp.array([n_p - 1]), n_p + dec_seq * jnp.arange(bd) + dec_seq - 1])
    rkv0 = None
    outs = []
    for l in range(depth):
        w_in_b = w_in[l].astype(BF16)
        xcat = norm_shift(x, norm_mix[l], jnp.repeat(state_shift[l], dec_seq, axis=0), n_p, dec_seq)
        xn_last = _rms_norm(x[last_rows], norm_mix[l])

        c0 = 3 * d_a
        w_rkv = w_in[l][:, :c0]
        rkv = matmul(xcat, jnp.concatenate([w_rkv * (1.0 - mu_rkv[l]), w_rkv * mu_rkv[l]], axis=0).astype(BF16))
        q_b = matmul(xcat, (w_in[l][:, c0:c0 + d_b] * (LOG2E * HEAD_DIM ** -0.5)).astype(BF16), (BF16,))
        k_s, k_b = matmul(xcat, w_in_b[:, c0 + d_b:c0 + 2 * d_b], (F32, BF16))
        v_s, v_b = matmul(xcat, w_in_b[:, c0 + 2 * d_b:c0 + 3 * d_b], (F32, BF16))
        p_gate = matmul(xcat, w_in_b[:, c0 + 3 * d_b:])

        firsts = [w1[l], a1[l], g1[l]]
        mus = [mu_x[l, 0], mu_x[l, 1], mu_x[l, 2]]
        seconds = [w2[l], a2[l], g2[l]]
        if l > 0:
            firsts.append(v1[l - 1])
            mus.append(mu_vr[l - 1])
            seconds.append(v2[l - 1])
        widths = [f.shape[1] for f in firsts]
        tot = sum(widths)
        pad = (-tot) % LANES
        top = jnp.concatenate([(1.0 - m)[:, None] * f for m, f in zip(mus, firsts)]
                              + [jnp.zeros((d_model, pad), F32)], axis=1)
        bot = jnp.concatenate([m[:, None] * f for m, f in zip(mus, firsts)]
                              + [jnp.zeros((d_model, pad), F32)], axis=1)
        w_l1 = jnp.concatenate([top, bot], axis=0).astype(BF16)
        offs = [0]
        for wd in widths:
            offs.append(offs[-1] + wd)
        h2 = matmul(xcat, w_l1, (BF16,), act=(offs[1], offs[2], offs[3]))
        w_l2 = jnp.zeros((tot + pad, len(seconds) * d_a), F32)
        for i, s2 in enumerate(seconds):
            w_l2 = w_l2.at[offs[i]:offs[i + 1], i * d_a:(i + 1) * d_a].set(s2)
        o2 = matmul(h2, w_l2.astype(BF16))

        zero_row = jnp.zeros((d_a,), F32)
        prep_params = jnp.stack([w0[l], a0[l], k_k[l], k_a[l], v0[l - 1] if l > 0 else zero_row,
                                 zero_row, zero_row, zero_row])
        prep = wkv_prep(rkv, o2, rkv0 if l > 0 else None, prep_params, d_a, tm)
        if l == 0:
            rkv0 = rkv
            (log_decay, k_mod, a_in, b_in), v_src, v_col = prep, rkv, 2
        else:
            (log_decay, k_mod, a_in, b_in, v_src), v_col = prep, 0
        scan_in = (rkv, log_decay, k_mod, v_src, a_in, b_in)
        cols = (0, 0, 0, v_col, 0, 0)
        s0_p = jnp.zeros((bp, n_heads_a // 2, LANES, LANES), F32)
        y_p, sf_p = wkv7(*scan_in, s0_p, row0=0, n_seq=bp, n_chunks=seq // WKV_CHUNK, c_len=WKV_CHUNK,
                         col_blocks=cols)
        y_s, sf_s = wkv7(*scan_in, _pair_states(state_wkv[l]), row0=n_p, n_seq=bd, n_chunks=1, c_len=dec_seq,
                         col_blocks=cols)
        post_params = jnp.stack([ln_x_w[l], ln_x_b[l], r_k[l].reshape(d_a)] + [zero_row] * 5)
        za = wkv_post(jnp.concatenate([y_p, y_s], axis=0), rkv, k_mod, v_src, v_col, o2, post_params, d_a, tm)

        bias2 = sb_bias[l].astype(F32) * LOG2E
        att_p = sb_prompt(q_b, k_b, v_b, bias2, seq_len=n_p)
        q_smp = q_b[n_p:].reshape(bd, dec_seq, n_heads_b, HEAD_DIM)
        head_eye = jnp.eye(n_heads_b, dtype=BF16)
        qbd = jnp.einsum('bthd,hg->bhtgd', q_smp, head_eye).reshape(bd, n_heads_b * dec_seq, d_b)
        bias_col = jnp.repeat(bias2, dec_seq)[:, None]
        pad_rows = ((0, 0), (0, PAGE_SIZE - dec_seq), (0, 0))
        k_new = jnp.pad(k_b[n_p:].reshape(bd, dec_seq, d_b), pad_rows)
        v_new = jnp.pad(v_b[n_p:].reshape(bd, dec_seq, d_b), pad_rows)
        att_s = sb_sample(qbd, bias_col, k_new, v_new, pool_k, pool_v, pt_flat + l * n_pool,
                          dec_seq=dec_seq, n_heads=n_heads_b)
        att = jnp.concatenate([att_p, att_s.reshape(n_s, d_b)], axis=0)

        merged = branch_merge(za, att, w_a[l].astype(BF16), w_b[l].astype(BF16), p_gate)
        x, xf = proj_residual(merged, w_o[l].astype(BF16), x, norm_ffn[l], emit_x=True, norm_dtype=BF16)

        w_gate_b = w_gate[l].astype(BF16)
        cs = state_conv[l]
        s_idx = (jnp.arange(n_s) % dec_seq)[:, None]
        ov1 = jnp.repeat(cs[:, 1], dec_seq, axis=0)
        ov2 = jnp.where(s_idx == 0, jnp.repeat(cs[:, 0], dec_seq, axis=0), ov1)
        cw = jnp.concatenate([conv_w[l], conv_b[l][None], jnp.zeros((4, d_ff), F32)], axis=0)
        hidden = conv_ffn_hidden(xf, w_up[l].astype(BF16), w_gate_b, cw, ov1, ov2, n_p, dec_seq)
        tail = jnp.concatenate([jnp.array([n_p - 2, n_p - 1]),
                                (last_rows[1:, None] + jnp.array([-1, 0])).reshape(-1)])
        g_tail = matmul(jnp.pad(xf[tail], ((0, LANES - tail.shape[0]), (0, 0))), w_gate_b)
        if l < depth - 1:
            x = proj_residual(hidden, w_down[l].astype(BF16), x, None, emit_x=True, norm_dtype=None)
        else:
            y = proj_residual(hidden, w_down[l].astype(BF16), x, norm_final, emit_x=False, norm_dtype=F32)

        outs.append(dict(
            k_p=k_s[:n_p].reshape(bp, seq, n_heads_b, HEAD_DIM),
            v_p=v_s[:n_p].reshape(bp, seq, n_heads_b, HEAD_DIM),
            shift_p=xn_last[:1],
            wkv_p=_unpair_states(sf_p),
            conv_p=g_tail[:2][None],
            k_s=k_s[n_p:].reshape(bd, dec_seq, n_heads_b, HEAD_DIM),
            v_s=v_s[n_p:].reshape(bd, dec_seq, n_heads_b, HEAD_DIM),
            shift_s=xn_last[1:],
            wkv_s=_unpair_states(sf_s),
            conv_s=g_tail[2:2 + 2 * bd].reshape(bd, 2, d_ff)))

    def stack(name):
        return jnp.stack([o[name] for o in outs])

    return (y[:n_p].reshape(bp, seq, d_model), y[n_p:].reshape(bd, dec_seq, d_model),
            stack('shift_p'), stack('wkv_p'), stack('conv_p'), stack('k_p'), stack('v_p'),
            stack('shift_s'), stack('wkv_s'), stack('conv_s'), stack('k_s'), stack('v_s'))
```

```python
import functools

import jax
import jax.numpy as jnp
from jax import lax
from jax.experimental import pallas as pl
from jax.experimental.pallas import tpu as pltpu

F32 = jnp.float32
BF16 = jnp.bfloat16

HEAD_DIM = 64
LANES = 128
PAGE_SIZE = 128
RMS_EPS = 1e-6
GN_EPS = 64e-5
LOG2E = 1.4426950408889634
WKV_CHUNK = 64
VMEM_LIMIT = 56 * 1024 * 1024

NN = (((1,), (0,)), ((), ()))
NT = (((1,), (1,)), ((), ()))
TN = (((0,), (0,)), ((), ()))


def _dot(a, b, dims=NN):
    return lax.dot_general(a, b, dims, preferred_element_type=F32)


def _split2(x):
    hi = x.astype(BF16)
    lo = (x - hi.astype(F32)).astype(BF16)
    return hi, lo


def _dot3(a, b, dims=NN):
    ah, al = _split2(a)
    bh, bl = _split2(b)
    return _dot(ah, bh, dims) + (_dot(ah, bl, dims) + _dot(al, bh, dims))


def _pick(dim, candidates):
    for c in candidates:
        if dim % c == 0:
            return c
    return dim


def _lora_act(h, bounds):
    col = lax.broadcasted_iota(jnp.int32, h.shape, 1)
    is_tanh = col < bounds[0]
    is_sig = jnp.logical_and(col >= bounds[1], col < bounds[2])
    return jnp.where(is_tanh, jnp.tanh(h), jnp.where(is_sig, jax.nn.sigmoid(h), h))


def _mm_kernel(x_ref, w_ref, *rest, nk, act):
    o_refs = rest[:-1] if nk > 1 else rest
    k = pl.program_id(2)

    def finish(acc):
        if act is not None:
            acc = _lora_act(acc, act)
        for o_ref in o_refs:
            o_ref[...] = acc.astype(o_ref.dtype)

    if nk == 1:
        finish(_dot(x_ref[...], w_ref[...]))
        return
    acc_ref = rest[-1]

    @pl.when(k == 0)
    def _():
        acc_ref[...] = jnp.zeros_like(acc_ref)

    acc_ref[...] += _dot(x_ref[...], w_ref[...])

    @pl.when(k == nk - 1)
    def _():
        finish(acc_ref[...])


def matmul(x, w, out_dtypes=(F32,), act=None):
    m = x.shape[0]
    kd, n = w.shape
    tm = _pick(m, (768, 512, 256, 128, 64, 32, 16, 8))
    tn = _pick(n, (1024, 512, 256, 128))
    tk = _pick(kd, (2048, 1024, 512, 256, 128))
    nk = kd // tk
    outs = pl.pallas_call(
        functools.partial(_mm_kernel, nk=nk, act=act),
        grid=(m // tm, n // tn, nk),
        in_specs=[pl.BlockSpec((tm, tk), lambda i, j, k: (i, k)),
                  pl.BlockSpec((tk, tn), lambda i, j, k: (k, j))],
        out_specs=[pl.BlockSpec((tm, tn), lambda i, j, k: (i, j))] * len(out_dtypes),
        out_shape=[jax.ShapeDtypeStruct((m, n), dt) for dt in out_dtypes],
        scratch_shapes=[pltpu.VMEM((tm, tn), F32)] if nk > 1 else [],
        compiler_params=pltpu.CompilerParams(
            dimension_semantics=("parallel", "parallel", "arbitrary"), vmem_limit_bytes=VMEM_LIMIT),
        name="mm")(x, w)
    return outs[0] if len(out_dtypes) == 1 else outs


def _bf(x):
    return x.astype(BF16)


def _wkv_chunk_terms(rs, lws, ks, vs, as_, bs, c_len):
    n = len(rs)
    c2 = 2 * c_len
    row = lax.broadcasted_iota(jnp.int32, (c_len, c_len), 0)
    col = lax.broadcasted_iota(jnp.int32, (c_len, c_len), 1)
    eye = (col == row).astype(F32)
    tri = (col <= row).astype(BF16)
    r2 = lax.broadcasted_iota(jnp.int32, (c2, c2), 0)
    j2 = lax.broadcasted_iota(jnp.int32, (c2, c2), 1) & (c_len - 1)
    t2 = r2 & (c_len - 1)
    keep = jnp.logical_or(j2 < t2, jnp.logical_and(r2 >= c_len, j2 == t2))
    lane = lax.broadcasted_iota(jnp.int32, (c_len, LANES), 1)
    m0 = lane < HEAD_DIM
    masks = (m0, jnp.logical_not(m0))
    rr = lax.broadcasted_iota(jnp.int32, (LANES, LANES), 0)
    cc = lax.broadcasted_iota(jnp.int32, (LANES, LANES), 1)
    same_head = (rr < HEAD_DIM) == (cc < HEAD_DIM)
    diag = rr == cc
    n_double = max(c_len.bit_length() - 2, 0)
    zeros = jnp.zeros((c_len, LANES), F32)

    ats, rts, qs, bks, wcs, vpads = [], [], [], [], [], []
    for p in range(n):
        lw = lws[p]
        l1 = _bf(lw)
        rem = lw - l1.astype(F32)
        l2 = _bf(rem)
        l3 = _bf(rem - l2.astype(F32))
        lam = _dot(tri, l1) + (_dot(tri, l2) + _dot(tri, l3))
        lam_c = lam[c_len - 1:c_len, :]
        e_neg = jnp.exp(-lam)
        e_end = jnp.exp(lam_c - lam)
        ats.append(as_[p] * jnp.exp(lam - lw))
        rts.append(rs[p] * jnp.exp(lam))
        qs.append(_bf(jnp.concatenate([bs[p] * e_neg, ks[p] * e_neg], axis=0)))
        bks.append(_bf(jnp.concatenate([bs[p] * e_end, ks[p] * e_end], axis=0)))
        wcs.append(jnp.exp(lam_c))
        vpads.append(_bf(jnp.concatenate([zeros, vs[p]], axis=0)))

    units = [(p, h) for p in range(n) for h in range(2)]
    ms = []
    for p, h in units:
        lhs = jnp.concatenate([jnp.where(masks[h], ats[p], 0.0), jnp.where(masks[h], rts[p], 0.0)], axis=0)
        ms.append(jnp.where(keep, _dot(_bf(lhs), qs[p], NT), 0.0))
    tops = [_bf(m[:c_len]) for m in ms]
    bots = [_bf(m[c_len:]) for m in ms]
    pws = [m[:c_len, :c_len] for m in ms]
    tinvs = [eye + x for x in pws]
    for _ in range(n_double):
        pws = [_dot(_bf(x), _bf(x)) for x in pws]
        tinvs = [t + _dot(_bf(t), _bf(x)) for t, x in zip(tinvs, pws)]
    tinvs = [_bf(t) for t in tinvs]
    at_b = [_bf(x) for x in ats]
    tas = [_dot(t, at_b[p]) for t, (p, h) in zip(tinvs, units)]
    lakv = [_dot(top, vpads[p]) for top, (p, h) in zip(tops, units)]
    u0s = [_dot(t, _bf(x)) for t, x in zip(tinvs, lakv)]
    ta_p = [_bf(jnp.where(m0, tas[2 * p], tas[2 * p + 1])) for p in range(n)]
    u0_p = [jnp.where(m0, u0s[2 * p], u0s[2 * p + 1]) for p in range(n)]
    uv_p = [_bf(jnp.concatenate([u0_p[p], vs[p]], axis=0)) for p in range(n)]
    rcs = [_dot(bot[:, :c_len], ta_p[p]) for bot, (p, h) in zip(bots, units)]
    y0s = [_dot(bot, uv_p[p]) for bot, (p, h) in zip(bots, units)]
    rc = [rts[p] + jnp.where(m0, rcs[2 * p], rcs[2 * p + 1]) for p in range(n)]
    y0 = [jnp.where(m0, y0s[2 * p], y0s[2 * p + 1]) for p in range(n)]
    pc = [jnp.where(same_head, _dot(bks[p][:c_len], ta_p[p], TN), 0.0)
          + jnp.where(diag, jnp.broadcast_to(wcs[p], (LANES, LANES)), 0.0) for p in range(n)]
    d0 = [jnp.where(same_head, _dot(bks[p], uv_p[p], TN), 0.0) for p in range(n)]
    return rc, y0, pc, d0


def _wkv_terms_kernel(r_ref, lw_ref, k_ref, v_ref, a_ref, b_ref, rc_ref, y0_ref, pc_ref, d0_ref,
                      *, c_len, pairs):
    sls = [slice(p * LANES, (p + 1) * LANES) for p in range(pairs)]
    rc, y0, pc, d0 = _wkv_chunk_terms(*[[ref[:, sl] for sl in sls]
                                        for ref in (r_ref, lw_ref, k_ref, v_ref, a_ref, b_ref)], c_len)
    for p, sl in enumerate(sls):
        rc_ref[:, sl] = rc[p]
        y0_ref[:, sl] = y0[p]
        pc_ref[0, p] = pc[p]
        d0_ref[0, p] = d0[p]


def _wkv_sweep_kernel(rc_ref, y0_ref, pc_ref, d0_ref, s0_ref, y_ref, sout_ref, st_ref, *, n_chunks, n_pairs):
    c = pl.program_id(1)

    @pl.when(c == 0)
    def _():
        st_ref[...] = s0_ref[0]

    for p in range(n_pairs):
        sl = slice(p * LANES, (p + 1) * LANES)
        st = st_ref[p]
        y_ref[:, sl] = _dot3(rc_ref[:, sl], st) + y0_ref[:, sl]
        st_ref[p] = _dot3(pc_ref[0, p], st) + d0_ref[0, p]

    @pl.when(c == n_chunks - 1)
    def _():
        sout_ref[0] = st_ref[...]


def wkv7(r, lw, k, v, a, b, s0, *, row0, n_seq, n_chunks, c_len, col_blocks=(0, 0, 0, 0, 0, 0)):
    width = lw.shape[1]
    n_pairs = width // LANES
    pairs = n_pairs
    tot_chunks = n_seq * n_chunks
    rows = tot_chunks * c_len
    blk0 = row0 // c_len
    lw_cols = pairs * LANES
    in_specs = [pl.BlockSpec((c_len, lw_cols), functools.partial(lambda c, g, cb: (blk0 + c, cb), cb=cb))
                for cb in col_blocks]
    row_spec = pl.BlockSpec((c_len, lw_cols), lambda c, g: (c, g))
    mat_spec = pl.BlockSpec((1, pairs, LANES, LANES), lambda c, g: (c, g, 0, 0))
    rc, y0, pc, d0 = pl.pallas_call(
        functools.partial(_wkv_terms_kernel, c_len=c_len, pairs=pairs),
        grid=(tot_chunks, n_pairs // pairs),
        in_specs=in_specs,
        out_specs=[row_spec, row_spec, mat_spec, mat_spec],
        out_shape=[jax.ShapeDtypeStruct((rows, width), F32), jax.ShapeDtypeStruct((rows, width), F32),
                   jax.ShapeDtypeStruct((tot_chunks, n_pairs, LANES, LANES), F32),
                   jax.ShapeDtypeStruct((tot_chunks, n_pairs, LANES, LANES), F32)],
        compiler_params=pltpu.CompilerParams(
            dimension_semantics=("parallel", "parallel"), vmem_limit_bytes=VMEM_LIMIT),
        name="wkv_terms")(r, lw, k, v, a, b)

    rowp = pl.BlockSpec((c_len, width), lambda s, c: (s * n_chunks + c, 0))
    matp = pl.BlockSpec((1, n_pairs, LANES, LANES), lambda s, c: (s * n_chunks + c, 0, 0, 0))
    stp = pl.BlockSpec((1, n_pairs, LANES, LANES), lambda s, c: (s, 0, 0, 0))
    y, s_fin = pl.pallas_call(
        functools.partial(_wkv_sweep_kernel, n_chunks=n_chunks, n_pairs=n_pairs),
        grid=(n_seq, n_chunks),
        in_specs=[rowp, rowp, matp, matp, stp],
        out_specs=[rowp, stp],
        out_shape=[jax.ShapeDtypeStruct((rows, width), F32),
                   jax.ShapeDtypeStruct((n_seq, n_pairs, LANES, LANES), F32)],
        scratch_shapes=[pltpu.VMEM((n_pairs, LANES, LANES), F32)],
        compiler_params=pltpu.CompilerParams(
            dimension_semantics=("parallel", "arbitrary"), vmem_limit_bytes=VMEM_LIMIT),
        name="wkv_sweep")(rc, y0, pc, d0, s0)
    return y, s_fin


def _tri_ones():
    jr = lax.broadcasted_iota(jnp.int32, (LANES, 2 * LANES), 0)
    jc = lax.broadcasted_iota(jnp.int32, (LANES, 2 * LANES), 1)
    return jnp.logical_or(jc >= LANES, jr >= jc).astype(BF16)


def _sb_sweep(zl, valid, carry, tri_ones):
    n_sub = zl.shape[1] // LANES
    sls = [slice(s * LANES, (s + 1) * LANES) for s in range(n_sub)]
    sps = []
    for sl in sls:
        z_s = zl[:, sl]
        sp = jnp.maximum(z_s, 0.0) + jnp.log(1.0 + jnp.exp2(-jnp.abs(z_s))) * LOG2E
        if valid is not None:
            sp = jnp.where(valid[:, sl], sp, 0.0)
        sps.append(sp.astype(BF16))
    exts = [_dot(sp, tri_ones) for sp in sps]
    atts = [None] * n_sub
    for s in reversed(range(n_sub)):
        att = jnp.exp2(zl[:, sls[s]] - (exts[s][:, :LANES] + carry))
        if valid is not None:
            att = jnp.where(valid[:, sls[s]], att, 0.0)
        atts[s] = att.astype(BF16)
        carry = carry + exts[s][:, LANES:]
    return (jnp.concatenate(atts, axis=1) if n_sub > 1 else atts[0]), carry


def _sb_prompt_kernel(q_ref, bias_ref, k_ref, v_ref, o_ref, *, tq, kb, n_par):
    i = pl.program_id(1)
    lane = lax.broadcasted_iota(jnp.int32, (tq, LANES), 1)
    m0 = lane < HEAD_DIM
    tri_ones = _tri_ones()
    lanes = [slice(p * LANES, (p + 1) * LANES) for p in range(n_par)]
    qs, biases = [], []
    for p, ls in enumerate(lanes):
        q = q_ref[:, ls]
        zero = jnp.zeros_like(q)
        qs.append(jnp.concatenate([jnp.where(m0, q, zero), jnp.where(m0, zero, q)], axis=0))
        biases.append(bias_ref[p])

    def logits(p, j):
        return _dot(qs[p], k_ref[pl.ds(pl.multiple_of(j * kb, kb), kb), lanes[p]], NT) + biases[p]

    def weighted(p, att, j):
        return _dot(att, v_ref[pl.ds(pl.multiple_of(j * kb, kb), kb), lanes[p]])

    jd = (i * tq) // kb
    row = lax.broadcasted_iota(jnp.int32, (2 * tq, kb), 0)
    col = lax.broadcasted_iota(jnp.int32, (2 * tq, kb), 1)
    valid = (jd * kb + col) < i * tq + jnp.where(row >= tq, row - tq, row)
    j0 = jnp.maximum(jd - 1, 0)
    init = []
    for p in range(n_par):
        att, carry = _sb_sweep(logits(p, jd), valid, jnp.zeros((2 * tq, LANES), F32), tri_ones)
        init.append((jnp.zeros((2 * tq, LANES), F32), carry, logits(p, j0), att))

    def body(jj, states):
        j = jd - 1 - jj
        out = []
        for p, (acc, carry, zl, att_prev) in enumerate(states):
            acc = acc + weighted(p, att_prev, j + 1)
            zl_next = logits(p, jnp.maximum(j - 1, 0))
            att, carry = _sb_sweep(zl, None, carry, tri_ones)
            out.append((acc, carry, zl_next, att))
        return tuple(out)

    final = lax.fori_loop(0, jd, body, tuple(init))
    for p, (acc, _, _, att) in enumerate(final):
        acc = acc + weighted(p, att, 0)
        o_ref[:, lanes[p]] = jnp.where(m0, acc[:tq], acc[tq:]).astype(o_ref.dtype)


def sb_prompt(q, k, v, bias, *, seq_len, tq=128, kb=512):
    width = q.shape[1]
    n_pairs = width // LANES
    nq = seq_len // tq
    n_par = 2 if n_pairs % 2 == 0 else 1
    assert kb % tq == 0 and seq_len % kb == 0
    bias_cols = jnp.repeat(bias.astype(F32).reshape(n_pairs, 2), tq, axis=1).reshape(n_pairs, 2 * tq, 1)
    wide = n_par * LANES
    return pl.pallas_call(
        functools.partial(_sb_prompt_kernel, tq=tq, kb=kb, n_par=n_par),
        grid=(n_pairs // n_par, nq),
        in_specs=[pl.BlockSpec((tq, wide), lambda p, i: (i, p)),
                  pl.BlockSpec((n_par, 2 * tq, 1), lambda p, i: (p, 0, 0)),
                  pl.BlockSpec((seq_len, wide), lambda p, i: (0, p)),
                  pl.BlockSpec((seq_len, wide), lambda p, i: (0, p))],
        out_specs=pl.BlockSpec((tq, wide), lambda p, i: (i, p)),
        out_shape=jax.ShapeDtypeStruct((seq_len, width), BF16),
        compiler_params=pltpu.CompilerParams(
            dimension_semantics=("parallel", "arbitrary"), vmem_limit_bytes=VMEM_LIMIT),
        name="sb_prompt")(q, bias_cols, k, v)


def _sb_sample_kernel(pt_ref, qbd_ref, bias_ref, kn_ref, vn_ref, *rest, n_steps, dec_seq, n_heads, n_pg):
    kp_refs, vp_refs = rest[:n_pg], rest[n_pg:2 * n_pg]
    o_ref, acc_ref, carry_ref = rest[2 * n_pg:]
    j = pl.program_id(1)
    rows = qbd_ref.shape[1]
    tri_ones = _tri_ones()
    qbd = qbd_ref[0]
    bias = bias_ref[...]

    @pl.when(j == 0)
    def _():
        row = lax.broadcasted_iota(jnp.int32, (rows, PAGE_SIZE), 0)
        col = lax.broadcasted_iota(jnp.int32, (rows, PAGE_SIZE), 1)
        zl = _dot(qbd, kn_ref[0], NT) + bias
        att, carry = _sb_sweep(zl, col < row % dec_seq, jnp.zeros((rows, LANES), F32), tri_ones)
        acc_ref[...] = _dot(att, vn_ref[0])
        carry_ref[...] = carry

    @pl.when(j > 0)
    def _():
        kt = jnp.concatenate([kp_refs[i][0].astype(BF16) for i in range(n_pg)], axis=1)
        vt = jnp.concatenate([vp_refs[i][0].astype(BF16) for i in range(n_pg)], axis=1)
        att, carry = _sb_sweep(_dot(qbd, kt) + bias, None, carry_ref[...], tri_ones)
        acc_ref[...] += _dot(att, vt, NT)
        carry_ref[...] = carry

    @pl.when(j == n_steps - 1)
    def _():
        lane = lax.broadcasted_iota(jnp.int32, (dec_seq, n_heads * HEAD_DIM), 1)
        out = jnp.zeros((dec_seq, n_heads * HEAD_DIM), F32)
        for h in range(n_heads):
            blk = acc_ref[h * dec_seq:(h + 1) * dec_seq, :]
            out = jnp.where(lane // HEAD_DIM == h, blk, out)
        o_ref[0] = out.astype(o_ref.dtype)


def sb_sample(qbd, bias_col, k_new, v_new, pool_kt, pool_vt, page_ids, *, dec_seq, n_heads, n_pg=8):
    bsz, rows, width = qbd.shape
    n_pages = page_ids.shape[0] // bsz
    assert n_pages % n_pg == 0
    n_steps = n_pages // n_pg + 1

    def pool_idx(i):
        return lambda b, j, pt: (pt[b * n_pages + n_pages - n_pg * jnp.maximum(j, 1) + i], 0, 0)

    page_specs = [pl.BlockSpec((1, width, PAGE_SIZE), pool_idx(i)) for i in range(n_pg)]
    grid_spec = pltpu.PrefetchScalarGridSpec(
        num_scalar_prefetch=1, grid=(bsz, n_steps),
        in_specs=[pl.BlockSpec((1, rows, width), lambda b, j, pt: (b, 0, 0)),
                  pl.BlockSpec((rows, 1), lambda b, j, pt: (0, 0)),
                  pl.BlockSpec((1, PAGE_SIZE, width), lambda b, j, pt: (b, 0, 0)),
                  pl.BlockSpec((1, PAGE_SIZE, width), lambda b, j, pt: (b, 0, 0))] + page_specs + page_specs,
        out_specs=pl.BlockSpec((1, dec_seq, width), lambda b, j, pt: (b, 0, 0)),
        scratch_shapes=[pltpu.VMEM((rows, width), F32), pltpu.VMEM((rows, PAGE_SIZE), F32)])
    return pl.pallas_call(
        functools.partial(_sb_sample_kernel, n_steps=n_steps, dec_seq=dec_seq, n_heads=n_heads, n_pg=n_pg),
        grid_spec=grid_spec,
        out_shape=jax.ShapeDtypeStruct((bsz, dec_seq, width), BF16),
        compiler_params=pltpu.CompilerParams(
            dimension_semantics=("parallel", "arbitrary"), vmem_limit_bytes=VMEM_LIMIT),
        name="sb_sample")(page_ids, qbd, bias_col, k_new, v_new, *([pool_kt] * n_pg), *([pool_vt] * n_pg))


def _row_rms(x, g):
    return x * lax.rsqrt(jnp.mean(x * x, axis=-1, keepdims=True) + RMS_EPS) * g


def _norm_shift_kernel(x_ref, xp_ref, g_ref, first_ref, o_ref, *, n_prompt_tiles, dec_seq):
    i = pl.program_id(0)
    g = g_ref[...]
    xn = _row_rms(x_ref[...], g)
    tm, d = xn.shape
    above = _row_rms(xp_ref[...], g)[7:8]
    is_sample = i >= n_prompt_tiles
    first = jnp.where(is_sample, first_ref[...],
                      jnp.broadcast_to(above, xn.shape) * jnp.where(i > 0, 1.0, 0.0))
    row = lax.broadcasted_iota(jnp.int32, (tm, 1), 0)
    starts = (row & jnp.where(is_sample, dec_seq - 1, -1)) == 0
    prev = jnp.where(starts, first, pltpu.roll(xn, 1, 0))
    o_ref[:, :d] = xn.astype(BF16)
    o_ref[:, d:] = prev.astype(BF16)


def norm_shift(x, gain, first_rows, n_prompt, dec_seq):
    m, d = x.shape
    tm = first_rows.shape[0]
    assert n_prompt % tm == 0 and m == n_prompt + tm and dec_seq & (dec_seq - 1) == 0
    return pl.pallas_call(
        functools.partial(_norm_shift_kernel, n_prompt_tiles=n_prompt // tm, dec_seq=dec_seq),
        grid=(m // tm,),
        in_specs=[pl.BlockSpec((tm, d), lambda i: (i, 0)),
                  pl.BlockSpec((8, d), lambda i: (jnp.maximum(i * (tm // 8) - 1, 0), 0)),
                  pl.BlockSpec((1, d), lambda i: (0, 0)),
                  pl.BlockSpec((tm, d), lambda i: (0, 0))],
        out_specs=pl.BlockSpec((tm, 2 * d), lambda i: (i, 0)),
        out_shape=jax.ShapeDtypeStruct((m, 2 * d), BF16),
        compiler_params=pltpu.CompilerParams(dimension_semantics=("parallel",), vmem_limit_bytes=VMEM_LIMIT),
        name="norm_shift")(x, x, gain.reshape(1, d), first_rows)


def _seg_ones():
    r = lax.broadcasted_iota(jnp.int32, (2 * LANES, LANES), 0) & (LANES - 1)
    c = lax.broadcasted_iota(jnp.int32, (2 * LANES, LANES), 1)
    return ((r < HEAD_DIM) == (c < HEAD_DIM)).astype(BF16)


def _head_sums(x, seg):
    parts = [_dot(jnp.concatenate(_split2(x[:, t:t + LANES]), axis=1), seg)
             for t in range(0, x.shape[1], LANES)]
    return jnp.concatenate(parts, axis=1)


def _wkv_prep_kernel(rkv_ref, o2_ref, *rest, d_a, has_v):
    if has_v:
        vf_ref, p_ref, lw_ref, k_ref, a_ref, b_ref, v_ref = rest
    else:
        p_ref, lw_ref, k_ref, a_ref, b_ref = rest
    p = p_ref[...]
    seg = _seg_ones()
    k = rkv_ref[:, d_a:2 * d_a]
    dw = p[0:1] + o2_ref[:, :d_a]
    w_log = -(jnp.maximum(-dw, 0.0) + jnp.log(1.0 + jnp.exp(-jnp.abs(dw)))) - 0.5
    lw_ref[...] = -jnp.exp(w_log)
    a_rate = jax.nn.sigmoid(p[1:2] + o2_ref[:, d_a:2 * d_a])
    kk = k * p[2:3]
    kk = kk * lax.rsqrt(_head_sums(kk * kk, seg) + 1e-12)
    k_ref[...] = k * (1.0 + (a_rate - 1.0) * p[3:4])
    a_ref[...] = -kk
    b_ref[...] = kk * a_rate
    if has_v:
        v = rkv_ref[:, 2 * d_a:]
        v_ref[...] = v + (vf_ref[...] - v) * jax.nn.sigmoid(p[4:5] + o2_ref[:, 3 * d_a:])


def wkv_prep(rkv, o2, v_first_src, params, d_a, tm):
    m = rkv.shape[0]
    has_v = v_first_src is not None
    row = pl.BlockSpec((tm, d_a), lambda i: (i, 0))
    in_specs = [pl.BlockSpec((tm, rkv.shape[1]), lambda i: (i, 0)),
                pl.BlockSpec((tm, o2.shape[1]), lambda i: (i, 0))]
    args = [rkv, o2]
    if has_v:
        in_specs.append(pl.BlockSpec((tm, d_a), lambda i: (i, 2)))
        args.append(v_first_src)
    in_specs.append(pl.BlockSpec(params.shape, lambda i: (0, 0)))
    n_out = 5 if has_v else 4
    return pl.pallas_call(
        functools.partial(_wkv_prep_kernel, d_a=d_a, has_v=has_v),
        grid=(m // tm,), in_specs=in_specs, out_specs=[row] * n_out,
        out_shape=[jax.ShapeDtypeStruct((m, d_a), F32)] * n_out,
        compiler_params=pltpu.CompilerParams(dimension_semantics=("parallel",), vmem_limit_bytes=VMEM_LIMIT),
        name="wkv_prep")(*args, params)


def _wkv_post_kernel(y_ref, r_ref, k_ref, v_ref, g_ref, p_ref, o_ref):
    p = p_ref[...]
    seg = _seg_ones()
    y = y_ref[...]
    inv = 1.0 / HEAD_DIM
    d = y - _head_sums(y, seg) * inv
    var = _head_sums(d * d, seg) * inv
    y_n = d * lax.rsqrt(var + GN_EPS) * p[0:1] + p[1:2]
    bonus = _head_sums(r_ref[...] * k_ref[...] * p[2:3], seg) * v_ref[...]
    o_ref[...] = ((y_n + bonus) * g_ref[...]).astype(BF16)


def wkv_post(y, rkv, k_mod, v_src, v_col, o2, params, d_a, tm):
    m = y.shape[0]

    def col(c):
        return pl.BlockSpec((tm, d_a), lambda i: (i, c))

    return pl.pallas_call(
        _wkv_post_kernel, grid=(m // tm,),
        in_specs=[col(0), col(0), col(0), col(v_col), col(2), pl.BlockSpec(params.shape, lambda i: (0, 0))],
        out_specs=col(0), out_shape=jax.ShapeDtypeStruct((m, d_a), BF16),
        compiler_params=pltpu.CompilerParams(dimension_semantics=("parallel",), vmem_limit_bytes=VMEM_LIMIT),
        name="wkv_post")(y, rkv, k_mod, v_src, o2, params)


def _merge_kernel(za_ref, att_ref, wa_ref, wb_ref, ga_ref, gb_ref, o_ref):
    a = _dot(za_ref[...], wa_ref[...])
    b = _dot(att_ref[...], wb_ref[...])
    o_ref[...] = (jax.nn.sigmoid(ga_ref[...]) * a + jax.nn.sigmoid(gb_ref[...]) * b).astype(BF16)


def branch_merge(za, att, w_a, w_b, p_gate):
    m, kd = za.shape
    n = w_a.shape[1]
    tm = _pick(m, (768, 512, 256, 128))
    tn = _pick(n, (1024, 512, 256, 128))
    nj = n // tn
    return pl.pallas_call(
        _merge_kernel, grid=(m // tm, nj),
        in_specs=[pl.BlockSpec((tm, kd), lambda i, j: (i, 0)), pl.BlockSpec((tm, kd), lambda i, j: (i, 0)),
                  pl.BlockSpec((kd, tn), lambda i, j: (0, j)), pl.BlockSpec((kd, tn), lambda i, j: (0, j)),
                  pl.BlockSpec((tm, tn), lambda i, j: (i, j)), pl.BlockSpec((tm, tn), lambda i, j: (i, nj + j))],
        out_specs=pl.BlockSpec((tm, tn), lambda i, j: (i, j)),
        out_shape=jax.ShapeDtypeStruct((m, n), BF16),
        compiler_params=pltpu.CompilerParams(
            dimension_semantics=("parallel", "parallel"), vmem_limit_bytes=VMEM_LIMIT),
        name="branch_merge")(za, att, w_a, w_b, p_gate, p_gate)


def _proj_res_kernel(h_ref, w_ref, x_ref, g_ref, *rest, nk, emit_x, norm_dtype):
    acc_ref = rest[-1]
    k = pl.program_id(1)

    @pl.when(k == 0)
    def _():
        acc_ref[...] = jnp.zeros_like(acc_ref)

    acc_ref[...] += _dot(h_ref[...], w_ref[...])

    @pl.when(k == nk - 1)
    def _():
        x_new = x_ref[...] + acc_ref[...]
        outs = list(rest[:-1])
        if emit_x:
            outs.pop(0)[...] = x_new
        if norm_dtype is not None:
            outs.pop(0)[...] = _row_rms(x_new, g_ref[...]).astype(norm_dtype)


def proj_residual(h, w, x, gain, *, emit_x, norm_dtype):
    m, kd = h.shape
    d = w.shape[1]
    tm = _pick(m, (384, 256, 128))
    tk = _pick(kd, (2048, 1024, 512))
    nk = kd // tk
    row = pl.BlockSpec((tm, d), lambda i, k: (i, 0))
    out_shape, out_specs = [], []
    if emit_x:
        out_shape.append(jax.ShapeDtypeStruct((m, d), F32))
        out_specs.append(row)
    if norm_dtype is not None:
        out_shape.append(jax.ShapeDtypeStruct((m, d), norm_dtype))
        out_specs.append(row)
    gain = jnp.ones((d,), F32) if gain is None else gain
    outs = pl.pallas_call(
        functools.partial(_proj_res_kernel, nk=nk, emit_x=emit_x, norm_dtype=norm_dtype),
        grid=(m // tm, nk),
        in_specs=[pl.BlockSpec((tm, tk), lambda i, k: (i, k)), pl.BlockSpec((tk, d), lambda i, k: (k, 0)),
                  row, pl.BlockSpec((1, d), lambda i, k: (0, 0))],
        out_specs=out_specs, out_shape=out_shape,
        scratch_shapes=[pltpu.VMEM((tm, d), F32)],
        compiler_params=pltpu.CompilerParams(
            dimension_semantics=("parallel", "arbitrary"), vmem_limit_bytes=VMEM_LIMIT),
        name="proj_residual")(h, w, x, gain.reshape(1, d))
    return outs[0] if len(outs) == 1 else outs


def _ffn_kernel(xf_ref, xp_ref, wu_ref, wg_ref, cw_ref, ov1_ref, ov2_ref, o_ref, *, n_tiles, sample_off, dec_seq):
    i = pl.program_id(1)
    xf = xf_ref[...]
    wg = wg_ref[...]
    u = _dot(xf, wu_ref[...])
    g = _dot(xf, wg)
    tm = g.shape[0]
    gp = _dot(xp_ref[...], wg) * jnp.where(i > 0, 1.0, 0.0)
    row = lax.broadcasted_iota(jnp.int32, (tm, 1), 0)
    g1 = jnp.where(row == 0, gp[7:8], pltpu.roll(g, 1, 0))
    g2 = jnp.where(row == 0, gp[6:7], jnp.where(row == 1, gp[7:8], pltpu.roll(g, 2, 0)))
    first_sample = jnp.where(i == n_tiles - 1, sample_off, tm)
    s = (row & (dec_seq - 1)) + jnp.where(row >= first_sample, 0, dec_seq)
    g1 = jnp.where(s == 0, ov1_ref[...], g1)
    g2 = jnp.where(s < 2, ov2_ref[...], g2)
    cw = cw_ref[...]
    conv = cw[0:1] * g2 + cw[1:2] * g1 + cw[2:3] * g + cw[3:4]
    o_ref[...] = (conv * jax.nn.sigmoid(conv) * u).astype(BF16)


def conv_ffn_hidden(xf, w_up, w_gate, cw, ov1, ov2, n_prompt, dec_seq):
    m, d = xf.shape
    d_ff = w_up.shape[1]
    n_s = ov1.shape[0]
    tm = next(t for t in (768, 512, 256, n_s) if m % t == 0 and (n_prompt % t) + n_s == t)
    off = n_prompt % tm
    tn = _pick(d_ff, (1024, 512, 256, 128))
    assert off % dec_seq == 0 and off % 8 == 0 and dec_seq >= 2
    ov1 = jnp.pad(ov1, ((off, 0), (0, 0)))
    ov2 = jnp.pad(ov2, ((off, 0), (0, 0)))
    wspec = pl.BlockSpec((d, tn), lambda j, i: (0, j))
    cspec = pl.BlockSpec((tm, tn), lambda j, i: (0, j))
    return pl.pallas_call(
        functools.partial(_ffn_kernel, n_tiles=m // tm, sample_off=off, dec_seq=dec_seq),
        grid=(d_ff // tn, m // tm),
        in_specs=[pl.BlockSpec((tm, d), lambda j, i: (i, 0)),
                  pl.BlockSpec((8, d), lambda j, i: (jnp.maximum(i * (tm // 8) - 1, 0), 0)),
                  wspec, wspec, pl.BlockSpec((8, tn), lambda j, i: (0, j)), cspec, cspec],
        out_specs=pl.BlockSpec((tm, tn), lambda j, i: (i, j)),
        out_shape=jax.ShapeDtypeStruct((m, d_ff), BF16),
        compiler_params=pltpu.CompilerParams(
            dimension_semantics=("parallel", "arbitrary"), vmem_limit_bytes=VMEM_LIMIT),
        name="conv_ffn")(xf, xf, w_up, w_gate, cw, ov1, ov2)


def _rms_norm(x, g):
    return x * lax.rsqrt(jnp.mean(x * x, axis=-1, keepdims=True) + RMS_EPS) * g


def _pair_states(s):
    bsz, n_heads = s.shape[:2]
    st = jnp.swapaxes(s, -1, -2).reshape(bsz, n_heads // 2, 2, HEAD_DIM, HEAD_DIM)
    z = jnp.zeros_like(st[:, :, 0])
    top = jnp.concatenate([st[:, :, 0], z], axis=-1)
    bot = jnp.concatenate([z, st[:, :, 1]], axis=-1)
    return jnp.concatenate([top, bot], axis=-2)


def _unpair_states(sp):
    bsz, n_pairs = sp.shape[:2]
    h0 = sp[:, :, :HEAD_DIM, :HEAD_DIM]
    h1 = sp[:, :, HEAD_DIM:, HEAD_DIM:]
    st = jnp.stack([h0, h1], axis=2).reshape(bsz, 2 * n_pairs, HEAD_DIM, HEAD_DIM)
    return jnp.swapaxes(st, -1, -2)


def kernel(x_prompt, x_sample, cache_sb_k, cache_sb_v, state_shift, state_wkv, state_conv, page_table, norm_mix, w_in, mu_rkv, mu_x, w0, w1, w2, a0, a1, a2, g1, g2, mu_vr, v0, v1, v2, k_k, k_a, r_k, ln_x_w, ln_x_b, w_a, sb_bias, w_b, w_o, norm_ffn, w_up, w_gate, conv_w, conv_b, w_down, norm_final):
    depth = w_in.shape[0]
    bp, seq, d_model = x_prompt.shape
    bd, dec_seq, _ = x_sample.shape
    assert bp == 1, "one prompt sequence is concatenated with the sample rows"
    n_pool = cache_sb_k.shape[1]
    n_heads_a = state_wkv.shape[2]
    d_a = n_heads_a * HEAD_DIM
    n_heads_b = cache_sb_k.shape[3]
    d_b = n_heads_b * HEAD_DIM
    d_ff = w_up.shape[2]
    n_p = bp * seq
    n_s = bd * dec_seq
    n_pages = page_table.shape[1]
    assert n_heads_b * dec_seq == LANES, "sample queries of all heads fill one 128-row tile"

    pool_k = jnp.transpose(cache_sb_k, (0, 1, 3, 4, 2)).reshape(depth * n_pool, d_b, PAGE_SIZE)
    pool_v = jnp.transpose(cache_sb_v, (0, 1, 3, 4, 2)).reshape(depth * n_pool, d_b, PAGE_SIZE)
    pt_flat = page_table.reshape(-1).astype(jnp.int32)

    x = jnp.concatenate([x_prompt.reshape(n_p, d_model), x_sample.reshape(n_s, d_model)], axis=0)
    tm = n_s
    last_rows = jnp.concatenate([jnp.array([n_p - 1]), n_p + dec_seq * jnp.arange(bd) + dec_seq - 1])
    rkv0 = None
    outs = []
    for l in range(depth):
        w_in_b = w_in[l].astype(BF16)
        xcat = norm_shift(x, norm_mix[l], jnp.repeat(state_shift[l], dec_seq, axis=0), n_p, dec_seq)
        xn_last = _rms_norm(x[last_rows], norm_mix[l])

        c0 = 3 * d_a
        w_rkv = w_in[l][:, :c0]
        rkv = matmul(xcat, jnp.concatenate([w_rkv * (1.0 - mu_rkv[l]), w_rkv * mu_rkv[l]], axis=0).astype(BF16))
        q_b = matmul(xcat, (w_in[l][:, c0:c0 + d_b] * (LOG2E * HEAD_DIM ** -0.5)).astype(BF16), (BF16,))
        k_s, k_b = matmul(xcat, w_in_b[:, c0 + d_b:c0 + 2 * d_b], (F32, BF16))
        v_s, v_b = matmul(xcat, w_in_b[:, c0 + 2 * d_b:c0 + 3 * d_b], (F32, BF16))
        p_gate = matmul(xcat, w_in_b[:, c0 + 3 * d_b:])

        firsts = [w1[l], a1[l], g1[l]]
        mus = [mu_x[l, 0], mu_x[l, 1], mu_x[l, 2]]
        seconds = [w2[l], a2[l], g2[l]]
        if l > 0:
            firsts.append(v1[l - 1])
            mus.append(mu_vr[l - 1])
            seconds.append(v2[l - 1])
        widths = [f.shape[1] for f in firsts]
        tot = sum(widths)
        pad = (-tot) % LANES
        top = jnp.concatenate([(1.0 - m)[:, None] * f for m, f in zip(mus, firsts)]
                              + [jnp.zeros((d_model, pad), F32)], axis=1)
        bot = jnp.concatenate([m[:, None] * f for m, f in zip(mus, firsts)]
                              + [jnp.zeros((d_model, pad), F32)], axis=1)
        w_l1 = jnp.concatenate([top, bot], axis=0).astype(BF16)
        offs = [0]
        for wd in widths:
            offs.append(offs[-1] + wd)
        h2 = matmul(xcat, w_l1, (BF16,), act=(offs[1], offs[2], offs[3]))
        w_l2 = jnp.zeros((tot + pad, len(seconds) * d_a), F32)
        for i, s2 in enumerate(seconds):
            w_l2 = w_l2.at[offs[i]:offs[i + 1], i * d_a:(i + 1) * d_a].set(s2)
        o2 = matmul(h2, w_l2.astype(BF16))

        zero_row = jnp.zeros((d_a,), F32)
        prep_params = jnp.stack([w0[l], a0[l], k_k[l], k_a[l], v0[l - 1] if l > 0 else zero_row,
                                 zero_row, zero_row, zero_row])
        prep = wkv_prep(rkv, o2, rkv0 if l > 0 else None, prep_params, d_a, tm)
        if l == 0:
            rkv0 = rkv
            (log_decay, k_mod, a_in, b_in), v_src, v_col = prep, rkv, 2
        else:
            (log_decay, k_mod, a_in, b_in, v_src), v_col = prep, 0
        scan_in = (rkv, log_decay, k_mod, v_src, a_in, b_in)
        cols = (0, 0, 0, v_col, 0, 0)
        s0_p = jnp.zeros((bp, n_heads_a // 2, LANES, LANES), F32)
        y_p, sf_p = wkv7(*scan_in, s0_p, row0=0, n_seq=bp, n_chunks=seq // WKV_CHUNK, c_len=WKV_CHUNK,
                         col_blocks=cols)
        y_s, sf_s = wkv7(*scan_in, _pair_states(state_wkv[l]), row0=n_p, n_seq=bd, n_chunks=1, c_len=dec_seq,
                         col_blocks=cols)
        post_params = jnp.stack([ln_x_w[l], ln_x_b[l], r_k[l].reshape(d_a)] + [zero_row] * 5)
        za = wkv_post(jnp.concatenate([y_p, y_s], axis=0), rkv, k_mod, v_src, v_col, o2, post_params, d_a, tm)

        bias2 = sb_bias[l].astype(F32) * LOG2E
        att_p = sb_prompt(q_b, k_b, v_b, bias2, seq_len=n_p)
        q_smp = q_b[n_p:].reshape(bd, dec_seq, n_heads_b, HEAD_DIM)
        head_eye = jnp.eye(n_heads_b, dtype=BF16)
        qbd = jnp.einsum('bthd,hg->bhtgd', q_smp, head_eye).reshape(bd, n_heads_b * dec_seq, d_b)
        bias_col = jnp.repeat(bias2, dec_seq)[:, None]
        pad_rows = ((0, 0), (0, PAGE_SIZE - dec_seq), (0, 0))
        k_new = jnp.pad(k_b[n_p:].reshape(bd, dec_seq, d_b), pad_rows)
        v_new = jnp.pad(v_b[n_p:].reshape(bd, dec_seq, d_b), pad_rows)
        att_s = sb_sample(qbd, bias_col, k_new, v_new, pool_k, pool_v, pt_flat + l * n_pool,
                          dec_seq=dec_seq, n_heads=n_heads_b)
        att = jnp.concatenate([att_p, att_s.reshape(n_s, d_b)], axis=0)

        merged = branch_merge(za, att, w_a[l].astype(BF16), w_b[l].astype(BF16), p_gate)
        x, xf = proj_residual(merged, w_o[l].astype(BF16), x, norm_ffn[l], emit_x=True, norm_dtype=BF16)

        w_gate_b = w_gate[l].astype(BF16)
        cs = state_conv[l]
        s_idx = (jnp.arange(n_s) % dec_seq)[:, None]
        ov1 = jnp.repeat(cs[:, 1], dec_seq, axis=0)
        ov2 = jnp.where(s_idx == 0, jnp.repeat(cs[:, 0], dec_seq, axis=0), ov1)
        cw = jnp.concatenate([conv_w[l], conv_b[l][None], jnp.zeros((4, d_ff), F32)], axis=0)
        hidden = conv_ffn_hidden(xf, w_up[l].astype(BF16), w_gate_b, cw, ov1, ov2, n_p, dec_seq)
        tail = jnp.concatenate([jnp.array([n_p - 2, n_p - 1]),
                                (last_rows[1:, None] + jnp.array([-1, 0])).reshape(-1)])
        g_tail = matmul(jnp.pad(xf[tail], ((0, LANES - tail.shape[0]), (0, 0))), w_gate_b)
        if l < depth - 1:
            x = proj_residual(hidden, w_down[l].astype(BF16), x, None, emit_x=True, norm_dtype=None)
        else:
            y = proj_residual(hidden, w_down[l].astype(BF16), x, norm_final, emit_x=False, norm_dtype=F32)

        outs.append(dict(
            k_p=k_s[:n_p].reshape(bp, seq, n_heads_b, HEAD_DIM),
            v_p=v_s[:n_p].reshape(bp, seq, n_heads_b, HEAD_DIM),
            shift_p=xn_last[:1],
            wkv_p=_unpair_states(sf_p),
            conv_p=g_tail[:2][None],
            k_s=k_s[n_p:].reshape(bd, dec_seq, n_heads_b, HEAD_DIM),
            v_s=v_s[n_p:].reshape(bd, dec_seq, n_heads_b, HEAD_DIM),
            shift_s=xn_last[1:],
            wkv_s=_unpair_states(sf_s),
            conv_s=g_tail[2:2 + 2 * bd].reshape(bd, 2, d_ff)))

    def stack(name):
        return jnp.stack([o[name] for o in outs])

    return (y[:n_p].reshape(bp, seq, d_model), y[n_p:].reshape(bd, dec_seq, d_model),
            stack('shift_p'), stack('wkv_p'), stack('conv_p'), stack('k_p'), stack('v_p'),
            stack('shift_s'), stack('wkv_s'), stack('conv_s'), stack('k_s'), stack('v_s'))
```

```python
import functools

import jax
import jax.numpy as jnp
from jax import lax
from jax.experimental import pallas as pl
from jax.experimental.pallas import tpu as pltpu

F32 = jnp.float32
BF16 = jnp.bfloat16

HEAD_DIM = 64
LANES = 128
PAGE_SIZE = 128
RMS_EPS = 1e-6
GN_EPS = 64e-5
LOG2E = 1.4426950408889634
WKV_CHUNK = 64
VMEM_LIMIT = 56 * 1024 * 1024

NN = (((1,), (0,)), ((), ()))
NT = (((1,), (1,)), ((), ()))
TN = (((0,), (0,)), ((), ()))


def _dot(a, b, dims=NN):
    return lax.dot_general(a, b, dims, preferred_element_type=F32)


def _split2(x):
    hi = x.astype(BF16)
    lo = (x - hi.astype(F32)).astype(BF16)
    return hi, lo


def _dot3(a, b, dims=NN):
    ah, al = _split2(a)
    bh, bl = _split2(b)
    return _dot(ah, bh, dims) + (_dot(ah, bl, dims) + _dot(al, bh, dims))


def _pick(dim, candidates):
    for c in candidates:
        if dim % c == 0:
            return c
    return dim


def _lora_act(h, bounds):
    col = lax.broadcasted_iota(jnp.int32, h.shape, 1)
    is_tanh = col < bounds[0]
    is_sig = jnp.logical_and(col >= bounds[1], col < bounds[2])
    return jnp.where(is_tanh, jnp.tanh(h), jnp.where(is_sig, jax.nn.sigmoid(h), h))


def _mm_kernel(x_ref, w_ref, *rest, nk, act):
    o_refs = rest[:-1] if nk > 1 else rest
    k = pl.program_id(2)

    def finish(acc):
        if act is not None:
            acc = _lora_act(acc, act)
        for o_ref in o_refs:
            o_ref[...] = acc.astype(o_ref.dtype)

    if nk == 1:
        finish(_dot(x_ref[...], w_ref[...]))
        return
    acc_ref = rest[-1]

    @pl.when(k == 0)
    def _():
        acc_ref[...] = jnp.zeros_like(acc_ref)

    acc_ref[...] += _dot(x_ref[...], w_ref[...])

    @pl.when(k == nk - 1)
    def _():
        finish(acc_ref[...])


def matmul(x, w, out_dtypes=(F32,), act=None):
    m = x.shape[0]
    kd, n = w.shape
    tm = _pick(m, (768, 512, 256, 128, 64, 32, 16, 8))
    tn = _pick(n, (1024, 512, 256, 128))
    tk = _pick(kd, (2048, 1024, 512, 256, 128))
    nk = kd // tk
    outs = pl.pallas_call(
        functools.partial(_mm_kernel, nk=nk, act=act),
        grid=(m // tm, n // tn, nk),
        in_specs=[pl.BlockSpec((tm, tk), lambda i, j, k: (i, k)),
                  pl.BlockSpec((tk, tn), lambda i, j, k: (k, j))],
        out_specs=[pl.BlockSpec((tm, tn), lambda i, j, k: (i, j))] * len(out_dtypes),
        out_shape=[jax.ShapeDtypeStruct((m, n), dt) for dt in out_dtypes],
        scratch_shapes=[pltpu.VMEM((tm, tn), F32)] if nk > 1 else [],
        compiler_params=pltpu.CompilerParams(
            dimension_semantics=("parallel", "parallel", "arbitrary"), vmem_limit_bytes=VMEM_LIMIT),
        name="mm")(x, w)
    return outs[0] if len(out_dtypes) == 1 else outs


def _bf(x):
    return x.astype(BF16)


def _wkv_chunk_terms(rs, lws, ks, vs, as_, bs, c_len):
    n = len(rs)
    c2 = 2 * c_len
    row = lax.broadcasted_iota(jnp.int32, (c_len, c_len), 0)
    col = lax.broadcasted_iota(jnp.int32, (c_len, c_len), 1)
    eye = (col == row).astype(F32)
    tri = (col <= row).astype(BF16)
    r2 = lax.broadcasted_iota(jnp.int32, (c2, c2), 0)
    j2 = lax.broadcasted_iota(jnp.int32, (c2, c2), 1) & (c_len - 1)
    t2 = r2 & (c_len - 1)
    keep = jnp.logical_or(j2 < t2, jnp.logical_and(r2 >= c_len, j2 == t2))
    lane = lax.broadcasted_iota(jnp.int32, (c_len, LANES), 1)
    m0 = lane < HEAD_DIM
    masks = (m0, jnp.logical_not(m0))
    rr = lax.broadcasted_iota(jnp.int32, (LANES, LANES), 0)
    cc = lax.broadcasted_iota(jnp.int32, (LANES, LANES), 1)
    same_head = (rr < HEAD_DIM) == (cc < HEAD_DIM)
    diag = rr == cc
    n_double = max(c_len.bit_length() - 2, 0)
    zeros = jnp.zeros((c_len, LANES), F32)

    ats, rts, qs, bks, wcs, vpads = [], [], [], [], [], []
    for p in range(n):
        lw = lws[p]
        l1 = _bf(lw)
        rem = lw - l1.astype(F32)
        l2 = _bf(rem)
        l3 = _bf(rem - l2.astype(F32))
        lam = _dot(tri, l1) + (_dot(tri, l2) + _dot(tri, l3))
        lam_c = lam[c_len - 1:c_len, :]
        e_neg = jnp.exp(-lam)
        e_end = jnp.exp(lam_c - lam)
        ats.append(as_[p] * jnp.exp(lam - lw))
        rts.append(rs[p] * jnp.exp(lam))
        qs.append(_bf(jnp.concatenate([bs[p] * e_neg, ks[p] * e_neg], axis=0)))
        bks.append(_bf(jnp.concatenate([bs[p] * e_end, ks[p] * e_end], axis=0)))
        wcs.append(jnp.exp(lam_c))
        vpads.append(_bf(jnp.concatenate([zeros, vs[p]], axis=0)))

    units = [(p, h) for p in range(n) for h in range(2)]
    ms = []
    for p, h in units:
        lhs = jnp.concatenate([jnp.where(masks[h], ats[p], 0.0), jnp.where(masks[h], rts[p], 0.0)], axis=0)
        ms.append(jnp.where(keep, _dot(_bf(lhs), qs[p], NT), 0.0))
    tops = [_bf(m[:c_len]) for m in ms]
    bots = [_bf(m[c_len:]) for m in ms]
    pws = [m[:c_len, :c_len] for m in ms]
    tinvs = [eye + x for x in pws]
    for _ in range(n_double):
        pws = [_dot(_bf(x), _bf(x)) for x in pws]
        tinvs = [t + _dot(_bf(t), _bf(x)) for t, x in zip(tinvs, pws)]
    tinvs = [_bf(t) for t in tinvs]
    at_b = [_bf(x) for x in ats]
    tas = [_dot(t, at_b[p]) for t, (p, h) in zip(tinvs, units)]
    lakv = [_dot(top, vpads[p]) for top, (p, h) in zip(tops, units)]
    u0s = [_dot(t, _bf(x)) for t, x in zip(tinvs, lakv)]
    ta_p = [_bf(jnp.where(m0, tas[2 * p], tas[2 * p + 1])) for p in range(n)]
    u0_p = [jnp.where(m0, u0s[2 * p], u0s[2 * p + 1]) for p in range(n)]
    uv_p = [_bf(jnp.concatenate([u0_p[p], vs[p]], axis=0)) for p in range(n)]
    rcs = [_dot(bot[:, :c_len], ta_p[p]) for bot, (p, h) in zip(bots, units)]
    y0s = [_dot(bot, uv_p[p]) for bot, (p, h) in zip(bots, units)]
    rc = [rts[p] + jnp.where(m0, rcs[2 * p], rcs[2 * p + 1]) for p in range(n)]
    y0 = [jnp.where(m0, y0s[2 * p], y0s[2 * p + 1]) for p in range(n)]
    pc = [jnp.where(same_head, _dot(bks[p][:c_len], ta_p[p], TN), 0.0)
          + jnp.where(diag, jnp.broadcast_to(wcs[p], (LANES, LANES)), 0.0) for p in range(n)]
    d0 = [jnp.where(same_head, _dot(bks[p], uv_p[p], TN), 0.0) for p in range(n)]
    return rc, y0, pc, d0


def _wkv_terms_kernel(r_ref, lw_ref, k_ref, v_ref, a_ref, b_ref, rc_ref, y0_ref, pc_ref, d0_ref,
                      *, c_len, pairs, cps):
    units = [(c, p, slice(c * c_len, (c + 1) * c_len), slice(p * LANES, (p + 1) * LANES))
             for c in range(cps) for p in range(pairs)]
    rc, y0, pc, d0 = _wkv_chunk_terms(*[[ref[rs, ls] for _, _, rs, ls in units]
                                        for ref in (r_ref, lw_ref, k_ref, v_ref, a_ref, b_ref)], c_len)
    for u, (c, p, rs, ls) in enumerate(units):
        rc_ref[rs, ls] = rc[u]
        y0_ref[rs, ls] = y0[u]
        pc_ref[c, p] = pc[u]
        d0_ref[c, p] = d0[u]


def _wkv_sweep_kernel(rc_ref, y0_ref, pc_ref, d0_ref, s0_ref, y_ref, sout_ref, st_ref, *, n_chunks, n_pairs):
    c = pl.program_id(1)

    @pl.when(c == 0)
    def _():
        st_ref[...] = s0_ref[0]

    for p in range(n_pairs):
        sl = slice(p * LANES, (p + 1) * LANES)
        st = st_ref[p]
        y_ref[:, sl] = _dot3(rc_ref[:, sl], st) + y0_ref[:, sl]
        st_ref[p] = _dot3(pc_ref[0, p], st) + d0_ref[0, p]

    @pl.when(c == n_chunks - 1)
    def _():
        sout_ref[0] = st_ref[...]


def wkv7(r, lw, k, v, a, b, s0, *, row0, n_seq, n_chunks, c_len, col_blocks=(0, 0, 0, 0, 0, 0)):
    width = lw.shape[1]
    n_pairs = width // LANES
    pairs = n_pairs
    tot_chunks = n_seq * n_chunks
    rows = tot_chunks * c_len
    cps = 2 if tot_chunks % 2 == 0 and row0 % (2 * c_len) == 0 else 1
    blk_rows = cps * c_len
    blk0 = row0 // blk_rows
    lw_cols = pairs * LANES
    in_specs = [pl.BlockSpec((blk_rows, lw_cols), functools.partial(lambda c, g, cb: (blk0 + c, cb), cb=cb))
                for cb in col_blocks]
    row_spec = pl.BlockSpec((blk_rows, lw_cols), lambda c, g: (c, g))
    mat_spec = pl.BlockSpec((cps, pairs, LANES, LANES), lambda c, g: (c, g, 0, 0))
    rc, y0, pc, d0 = pl.pallas_call(
        functools.partial(_wkv_terms_kernel, c_len=c_len, pairs=pairs, cps=cps),
        grid=(tot_chunks // cps, n_pairs // pairs),
        in_specs=in_specs,
        out_specs=[row_spec, row_spec, mat_spec, mat_spec],
        out_shape=[jax.ShapeDtypeStruct((rows, width), F32), jax.ShapeDtypeStruct((rows, width), F32),
                   jax.ShapeDtypeStruct((tot_chunks, n_pairs, LANES, LANES), F32),
                   jax.ShapeDtypeStruct((tot_chunks, n_pairs, LANES, LANES), F32)],
        compiler_params=pltpu.CompilerParams(
            dimension_semantics=("parallel", "parallel"), vmem_limit_bytes=VMEM_LIMIT),
        name="wkv_terms")(r, lw, k, v, a, b)

    rowp = pl.BlockSpec((c_len, width), lambda s, c: (s * n_chunks + c, 0))
    matp = pl.BlockSpec((1, n_pairs, LANES, LANES), lambda s, c: (s * n_chunks + c, 0, 0, 0))
    stp = pl.BlockSpec((1, n_pairs, LANES, LANES), lambda s, c: (s, 0, 0, 0))
    y, s_fin = pl.pallas_call(
        functools.partial(_wkv_sweep_kernel, n_chunks=n_chunks, n_pairs=n_pairs),
        grid=(n_seq, n_chunks),
        in_specs=[rowp, rowp, matp, matp, stp],
        out_specs=[rowp, stp],
        out_shape=[jax.ShapeDtypeStruct((rows, width), F32),
                   jax.ShapeDtypeStruct((n_seq, n_pairs, LANES, LANES), F32)],
        scratch_shapes=[pltpu.VMEM((n_pairs, LANES, LANES), F32)],
        compiler_params=pltpu.CompilerParams(
            dimension_semantics=("parallel", "arbitrary"), vmem_limit_bytes=VMEM_LIMIT),
        name="wkv_sweep")(rc, y0, pc, d0, s0)
    return y, s_fin


def _tri_ones():
    jr = lax.broadcasted_iota(jnp.int32, (LANES, 2 * LANES), 0)
    jc = lax.broadcasted_iota(jnp.int32, (LANES, 2 * LANES), 1)
    return jnp.logical_or(jc >= LANES, jr >= jc).astype(BF16)


def _sb_sweep(zl, valid, carry, tri_ones):
    n_sub = zl.shape[1] // LANES
    sls = [slice(s * LANES, (s + 1) * LANES) for s in range(n_sub)]
    sps = []
    for sl in sls:
        z_s = zl[:, sl]
        sp = jnp.maximum(z_s, 0.0) + jnp.log(1.0 + jnp.exp2(-jnp.abs(z_s))) * LOG2E
        if valid is not None:
            sp = jnp.where(valid[:, sl], sp, 0.0)
        sps.append(sp.astype(BF16))
    exts = [_dot(sp, tri_ones) for sp in sps]
    atts = [None] * n_sub
    for s in reversed(range(n_sub)):
        att = jnp.exp2(zl[:, sls[s]] - (exts[s][:, :LANES] + carry))
        if valid is not None:
            att = jnp.where(valid[:, sls[s]], att, 0.0)
        atts[s] = att.astype(BF16)
        carry = carry + exts[s][:, LANES:]
    return (jnp.concatenate(atts, axis=1) if n_sub > 1 else atts[0]), carry


def _sb_prompt_kernel(q_ref, bias_ref, k_ref, v_ref, o_ref, *, tq, kb, n_par):
    i = pl.program_id(1)
    lane = lax.broadcasted_iota(jnp.int32, (tq, LANES), 1)
    m0 = lane < HEAD_DIM
    tri_ones = _tri_ones()
    lanes = [slice(p * LANES, (p + 1) * LANES) for p in range(n_par)]
    qs, biases = [], []
    for p, ls in enumerate(lanes):
        q = q_ref[:, ls]
        zero = jnp.zeros_like(q)
        qs.append(jnp.concatenate([jnp.where(m0, q, zero), jnp.where(m0, zero, q)], axis=0))
        biases.append(bias_ref[p])

    def logits(p, j):
        return _dot(qs[p], k_ref[pl.ds(pl.multiple_of(j * kb, kb), kb), lanes[p]], NT) + biases[p]

    def weighted(p, att, j):
        return _dot(att, v_ref[pl.ds(pl.multiple_of(j * kb, kb), kb), lanes[p]])

    jd = (i * tq) // kb
    row = lax.broadcasted_iota(jnp.int32, (2 * tq, kb), 0)
    col = lax.broadcasted_iota(jnp.int32, (2 * tq, kb), 1)
    valid = (jd * kb + col) < i * tq + jnp.where(row >= tq, row - tq, row)
    j0 = jnp.maximum(jd - 1, 0)
    init = []
    for p in range(n_par):
        att, carry = _sb_sweep(logits(p, jd), valid, jnp.zeros((2 * tq, LANES), F32), tri_ones)
        init.append((jnp.zeros((2 * tq, LANES), F32), carry, logits(p, j0), att))

    def body(jj, states):
        j = jd - 1 - jj
        out = []
        for p, (acc, carry, zl, att_prev) in enumerate(states):
            acc = acc + weighted(p, att_prev, j + 1)
            zl_next = logits(p, jnp.maximum(j - 1, 0))
            att, carry = _sb_sweep(zl, None, carry, tri_ones)
            out.append((acc, carry, zl_next, att))
        return tuple(out)

    final = lax.fori_loop(0, jd, body, tuple(init))
    for p, (acc, _, _, att) in enumerate(final):
        acc = acc + weighted(p, att, 0)
        o_ref[:, lanes[p]] = jnp.where(m0, acc[:tq], acc[tq:]).astype(o_ref.dtype)


def sb_prompt(q, k, v, bias, *, seq_len, tq=128, kb=512):
    width = q.shape[1]
    n_pairs = width // LANES
    nq = seq_len // tq
    n_par = next(n for n in (4, 2, 1) if n_pairs % n == 0)
    assert kb % tq == 0 and seq_len % kb == 0
    bias_cols = jnp.repeat(bias.astype(F32).reshape(n_pairs, 2), tq, axis=1).reshape(n_pairs, 2 * tq, 1)
    wide = n_par * LANES
    return pl.pallas_call(
        functools.partial(_sb_prompt_kernel, tq=tq, kb=kb, n_par=n_par),
        grid=(n_pairs // n_par, nq),
        in_specs=[pl.BlockSpec((tq, wide), lambda p, i: (i, p)),
                  pl.BlockSpec((n_par, 2 * tq, 1), lambda p, i: (p, 0, 0)),
                  pl.BlockSpec((seq_len, wide), lambda p, i: (0, p)),
                  pl.BlockSpec((seq_len, wide), lambda p, i: (0, p))],
        out_specs=pl.BlockSpec((tq, wide), lambda p, i: (i, p)),
        out_shape=jax.ShapeDtypeStruct((seq_len, width), BF16),
        compiler_params=pltpu.CompilerParams(
            dimension_semantics=("parallel", "arbitrary"), vmem_limit_bytes=VMEM_LIMIT),
        name="sb_prompt")(q, bias_cols, k, v)


def _sb_sample_kernel(pt_ref, qbd_ref, bias_ref, kn_ref, vn_ref, *rest, n_steps, dec_seq, n_heads, n_pg):
    kp_refs, vp_refs = rest[:n_pg], rest[n_pg:2 * n_pg]
    o_ref, acc_ref, carry_ref = rest[2 * n_pg:]
    j = pl.program_id(1)
    rows = qbd_ref.shape[1]
    tri_ones = _tri_ones()
    qbd = qbd_ref[0]
    bias = bias_ref[...]

    @pl.when(j == 0)
    def _():
        row = lax.broadcasted_iota(jnp.int32, (rows, PAGE_SIZE), 0)
        col = lax.broadcasted_iota(jnp.int32, (rows, PAGE_SIZE), 1)
        zl = _dot(qbd, kn_ref[0], NT) + bias
        att, carry = _sb_sweep(zl, col < row % dec_seq, jnp.zeros((rows, LANES), F32), tri_ones)
        acc_ref[...] = _dot(att, vn_ref[0])
        carry_ref[...] = carry

    @pl.when(j > 0)
    def _():
        kt = jnp.concatenate([kp_refs[i][0].astype(BF16) for i in range(n_pg)], axis=1)
        vt = jnp.concatenate([vp_refs[i][0].astype(BF16) for i in range(n_pg)], axis=1)
        att, carry = _sb_sweep(_dot(qbd, kt) + bias, None, carry_ref[...], tri_ones)
        acc_ref[...] += _dot(att, vt, NT)
        carry_ref[...] = carry

    @pl.when(j == n_steps - 1)
    def _():
        lane = lax.broadcasted_iota(jnp.int32, (dec_seq, n_heads * HEAD_DIM), 1)
        out = jnp.zeros((dec_seq, n_heads * HEAD_DIM), F32)
        for h in range(n_heads):
            blk = acc_ref[h * dec_seq:(h + 1) * dec_seq, :]
            out = jnp.where(lane // HEAD_DIM == h, blk, out)
        o_ref[0] = out.astype(o_ref.dtype)


def sb_sample(qbd, bias_col, k_new, v_new, pool_kt, pool_vt, page_ids, *, dec_seq, n_heads, n_pg=8):
    bsz, rows, width = qbd.shape
    n_pages = page_ids.shape[0] // bsz
    assert n_pages % n_pg == 0
    n_steps = n_pages // n_pg + 1

    def pool_idx(i):
        return lambda b, j, pt: (pt[b * n_pages + n_pages - n_pg * jnp.maximum(j, 1) + i], 0, 0)

    page_specs = [pl.BlockSpec((1, width, PAGE_SIZE), pool_idx(i)) for i in range(n_pg)]
    grid_spec = pltpu.PrefetchScalarGridSpec(
        num_scalar_prefetch=1, grid=(bsz, n_steps),
        in_specs=[pl.BlockSpec((1, rows, width), lambda b, j, pt: (b, 0, 0)),
                  pl.BlockSpec((rows, 1), lambda b, j, pt: (0, 0)),
                  pl.BlockSpec((1, PAGE_SIZE, width), lambda b, j, pt: (b, 0, 0)),
                  pl.BlockSpec((1, PAGE_SIZE, width), lambda b, j, pt: (b, 0, 0))] + page_specs + page_specs,
        out_specs=pl.BlockSpec((1, dec_seq, width), lambda b, j, pt: (b, 0, 0)),
        scratch_shapes=[pltpu.VMEM((rows, width), F32), pltpu.VMEM((rows, PAGE_SIZE), F32)])
    return pl.pallas_call(
        functools.partial(_sb_sample_kernel, n_steps=n_steps, dec_seq=dec_seq, n_heads=n_heads, n_pg=n_pg),
        grid_spec=grid_spec,
        out_shape=jax.ShapeDtypeStruct((bsz, dec_seq, width), BF16),
        compiler_params=pltpu.CompilerParams(
            dimension_semantics=("parallel", "arbitrary"), vmem_limit_bytes=VMEM_LIMIT),
        name="sb_sample")(page_ids, qbd, bias_col, k_new, v_new, *([pool_kt] * n_pg), *([pool_vt] * n_pg))


def _row_rms(x, g):
    return x * lax.rsqrt(jnp.mean(x * x, axis=-1, keepdims=True) + RMS_EPS) * g


def _norm_shift_kernel(x_ref, xp_ref, g_ref, first_ref, o_ref, *, n_prompt_tiles, dec_seq):
    i = pl.program_id(0)
    g = g_ref[...]
    xn = _row_rms(x_ref[...], g)
    tm, d = xn.shape
    above = _row_rms(xp_ref[...], g)[7:8]
    is_sample = i >= n_prompt_tiles
    first = jnp.where(is_sample, first_ref[...],
                      jnp.broadcast_to(above, xn.shape) * jnp.where(i > 0, 1.0, 0.0))
    row = lax.broadcasted_iota(jnp.int32, (tm, 1), 0)
    starts = (row & jnp.where(is_sample, dec_seq - 1, -1)) == 0
    prev = jnp.where(starts, first, pltpu.roll(xn, 1, 0))
    o_ref[:, :d] = xn.astype(BF16)
    o_ref[:, d:] = prev.astype(BF16)


def norm_shift(x, gain, first_rows, n_prompt, dec_seq):
    m, d = x.shape
    tm = first_rows.shape[0]
    assert n_prompt % tm == 0 and m == n_prompt + tm and dec_seq & (dec_seq - 1) == 0
    return pl.pallas_call(
        functools.partial(_norm_shift_kernel, n_prompt_tiles=n_prompt // tm, dec_seq=dec_seq),
        grid=(m // tm,),
        in_specs=[pl.BlockSpec((tm, d), lambda i: (i, 0)),
                  pl.BlockSpec((8, d), lambda i: (jnp.maximum(i * (tm // 8) - 1, 0), 0)),
                  pl.BlockSpec((1, d), lambda i: (0, 0)),
                  pl.BlockSpec((tm, d), lambda i: (0, 0))],
        out_specs=pl.BlockSpec((tm, 2 * d), lambda i: (i, 0)),
        out_shape=jax.ShapeDtypeStruct((m, 2 * d), BF16),
        compiler_params=pltpu.CompilerParams(dimension_semantics=("parallel",), vmem_limit_bytes=VMEM_LIMIT),
        name="norm_shift")(x, x, gain.reshape(1, d), first_rows)


def _seg_ones():
    r = lax.broadcasted_iota(jnp.int32, (2 * LANES, LANES), 0) & (LANES - 1)
    c = lax.broadcasted_iota(jnp.int32, (2 * LANES, LANES), 1)
    return ((r < HEAD_DIM) == (c < HEAD_DIM)).astype(BF16)


def _head_sums(x, seg):
    parts = [_dot(jnp.concatenate(_split2(x[:, t:t + LANES]), axis=1), seg)
             for t in range(0, x.shape[1], LANES)]
    return jnp.concatenate(parts, axis=1)


def _wkv_prep_kernel(rkv_ref, o2_ref, *rest, d_a, has_v):
    if has_v:
        vf_ref, p_ref, lw_ref, k_ref, a_ref, b_ref, v_ref = rest
    else:
        p_ref, lw_ref, k_ref, a_ref, b_ref = rest
    p = p_ref[...]
    seg = _seg_ones()
    k = rkv_ref[:, d_a:2 * d_a]
    dw = p[0:1] + o2_ref[:, :d_a]
    w_log = -(jnp.maximum(-dw, 0.0) + jnp.log(1.0 + jnp.exp(-jnp.abs(dw)))) - 0.5
    lw_ref[...] = -jnp.exp(w_log)
    a_rate = jax.nn.sigmoid(p[1:2] + o2_ref[:, d_a:2 * d_a])
    kk = k * p[2:3]
    kk = kk * lax.rsqrt(_head_sums(kk * kk, seg) + 1e-12)
    k_ref[...] = k * (1.0 + (a_rate - 1.0) * p[3:4])
    a_ref[...] = -kk
    b_ref[...] = kk * a_rate
    if has_v:
        v = rkv_ref[:, 2 * d_a:]
        v_ref[...] = v + (vf_ref[...] - v) * jax.nn.sigmoid(p[4:5] + o2_ref[:, 3 * d_a:])


def wkv_prep(rkv, o2, v_first_src, params, d_a, tm):
    m = rkv.shape[0]
    has_v = v_first_src is not None
    row = pl.BlockSpec((tm, d_a), lambda i: (i, 0))
    in_specs = [pl.BlockSpec((tm, rkv.shape[1]), lambda i: (i, 0)),
                pl.BlockSpec((tm, o2.shape[1]), lambda i: (i, 0))]
    args = [rkv, o2]
    if has_v:
        in_specs.append(pl.BlockSpec((tm, d_a), lambda i: (i, 2)))
        args.append(v_first_src)
    in_specs.append(pl.BlockSpec(params.shape, lambda i: (0, 0)))
    n_out = 5 if has_v else 4
    return pl.pallas_call(
        functools.partial(_wkv_prep_kernel, d_a=d_a, has_v=has_v),
        grid=(m // tm,), in_specs=in_specs, out_specs=[row] * n_out,
        out_shape=[jax.ShapeDtypeStruct((m, d_a), F32)] * n_out,
        compiler_params=pltpu.CompilerParams(dimension_semantics=("parallel",), vmem_limit_bytes=VMEM_LIMIT),
        name="wkv_prep")(*args, params)


def _wkv_post_kernel(y_ref, r_ref, k_ref, v_ref, g_ref, p_ref, o_ref):
    p = p_ref[...]
    seg = _seg_ones()
    y = y_ref[...]
    inv = 1.0 / HEAD_DIM
    d = y - _head_sums(y, seg) * inv
    var = _head_sums(d * d, seg) * inv
    y_n = d * lax.rsqrt(var + GN_EPS) * p[0:1] + p[1:2]
    bonus = _head_sums(r_ref[...] * k_ref[...] * p[2:3], seg) * v_ref[...]
    o_ref[...] = ((y_n + bonus) * g_ref[...]).astype(BF16)


def wkv_post(y, rkv, k_mod, v_src, v_col, o2, params, d_a, tm):
    m = y.shape[0]

    def col(c):
        return pl.BlockSpec((tm, d_a), lambda i: (i, c))

    return pl.pallas_call(
        _wkv_post_kernel, grid=(m // tm,),
        in_specs=[col(0), col(0), col(0), col(v_col), col(2), pl.BlockSpec(params.shape, lambda i: (0, 0))],
        out_specs=col(0), out_shape=jax.ShapeDtypeStruct((m, d_a), BF16),
        compiler_params=pltpu.CompilerParams(dimension_semantics=("parallel",), vmem_limit_bytes=VMEM_LIMIT),
        name="wkv_post")(y, rkv, k_mod, v_src, o2, params)


def _merge_kernel(za_ref, att_ref, wa_ref, wb_ref, ga_ref, gb_ref, o_ref):
    a = _dot(za_ref[...], wa_ref[...])
    b = _dot(att_ref[...], wb_ref[...])
    o_ref[...] = (jax.nn.sigmoid(ga_ref[...]) * a + jax.nn.sigmoid(gb_ref[...]) * b).astype(BF16)


def branch_merge(za, att, w_a, w_b, p_gate):
    m, kd = za.shape
    n = w_a.shape[1]
    tm = _pick(m, (768, 512, 256, 128))
    tn = _pick(n, (1024, 512, 256, 128))
    nj = n // tn
    return pl.pallas_call(
        _merge_kernel, grid=(m // tm, nj),
        in_specs=[pl.BlockSpec((tm, kd), lambda i, j: (i, 0)), pl.BlockSpec((tm, kd), lambda i, j: (i, 0)),
                  pl.BlockSpec((kd, tn), lambda i, j: (0, j)), pl.BlockSpec((kd, tn), lambda i, j: (0, j)),
                  pl.BlockSpec((tm, tn), lambda i, j: (i, j)), pl.BlockSpec((tm, tn), lambda i, j: (i, nj + j))],
        out_specs=pl.BlockSpec((tm, tn), lambda i, j: (i, j)),
        out_shape=jax.ShapeDtypeStruct((m, n), BF16),
        compiler_params=pltpu.CompilerParams(
            dimension_semantics=("parallel", "parallel"), vmem_limit_bytes=VMEM_LIMIT),
        name="branch_merge")(za, att, w_a, w_b, p_gate, p_gate)


def _proj_res_kernel(h_ref, w_ref, x_ref, g_ref, *rest, nk, emit_x, norm_dtype):
    acc_ref = rest[-1]
    k = pl.program_id(1)

    @pl.when(k == 0)
    def _():
        acc_ref[...] = jnp.zeros_like(acc_ref)

    acc_ref[...] += _dot(h_ref[...], w_ref[...])

    @pl.when(k == nk - 1)
    def _():
        x_new = x_ref[...] + acc_ref[...]
        outs = list(rest[:-1])
        if emit_x:
            outs.pop(0)[...] = x_new
        if norm_dtype is not None:
            outs.pop(0)[...] = _row_rms(x_new, g_ref[...]).astype(norm_dtype)


def proj_residual(h, w, x, gain, *, emit_x, norm_dtype):
    m, kd = h.shape
    d = w.shape[1]
    tm = _pick(m, (384, 256, 128))
    tk = _pick(kd, (2048, 1024, 512))
    nk = kd // tk
    row = pl.BlockSpec((tm, d), lambda i, k: (i, 0))
    out_shape, out_specs = [], []
    if emit_x:
        out_shape.append(jax.ShapeDtypeStruct((m, d), F32))
        out_specs.append(row)
    if norm_dtype is not None:
        out_shape.append(jax.ShapeDtypeStruct((m, d), norm_dtype))
        out_specs.append(row)
    gain = jnp.ones((d,), F32) if gain is None else gain
    outs = pl.pallas_call(
        functools.partial(_proj_res_kernel, nk=nk, emit_x=emit_x, norm_dtype=norm_dtype),
        grid=(m // tm, nk),
        in_specs=[pl.BlockSpec((tm, tk), lambda i, k: (i, k)), pl.BlockSpec((tk, d), lambda i, k: (k, 0)),
                  row, pl.BlockSpec((1, d), lambda i, k: (0, 0))],
        out_specs=out_specs, out_shape=out_shape,
        scratch_shapes=[pltpu.VMEM((tm, d), F32)],
        compiler_params=pltpu.CompilerParams(
            dimension_semantics=("parallel", "arbitrary"), vmem_limit_bytes=VMEM_LIMIT),
        name="proj_residual")(h, w, x, gain.reshape(1, d))
    return outs[0] if len(outs) == 1 else outs


def _ffn_kernel(xf_ref, xp_ref, wu_ref, wg_ref, cw_ref, ov1_ref, ov2_ref, o_ref, *, n_tiles, sample_off, dec_seq):
    i = pl.program_id(1)
    xf = xf_ref[...]
    wg = wg_ref[...]
    u = _dot(xf, wu_ref[...])
    g = _dot(xf, wg)
    tm = g.shape[0]
    gp = _dot(xp_ref[...], wg) * jnp.where(i > 0, 1.0, 0.0)
    row = lax.broadcasted_iota(jnp.int32, (tm, 1), 0)
    g1 = jnp.where(row == 0, gp[7:8], pltpu.roll(g, 1, 0))
    g2 = jnp.where(row == 0, gp[6:7], jnp.where(row == 1, gp[7:8], pltpu.roll(g, 2, 0)))
    first_sample = jnp.where(i == n_tiles - 1, sample_off, tm)
    s = (row & (dec_seq - 1)) + jnp.where(row >= first_sample, 0, dec_seq)
    g1 = jnp.where(s == 0, ov1_ref[...], g1)
    g2 = jnp.where(s < 2, ov2_ref[...], g2)
    cw = cw_ref[...]
    conv = cw[0:1] * g2 + cw[1:2] * g1 + cw[2:3] * g + cw[3:4]
    o_ref[...] = (conv * jax.nn.sigmoid(conv) * u).astype(BF16)


def conv_ffn_hidden(xf, w_up, w_gate, cw, ov1, ov2, n_prompt, dec_seq):
    m, d = xf.shape
    d_ff = w_up.shape[1]
    n_s = ov1.shape[0]
    tm = next(t for t in (768, 512, 256, n_s) if m % t == 0 and (n_prompt % t) + n_s == t)
    off = n_prompt % tm
    tn = _pick(d_ff, (1024, 512, 256, 128))
    assert off % dec_seq == 0 and off % 8 == 0 and dec_seq >= 2
    ov1 = jnp.pad(ov1, ((off, 0), (0, 0)))
    ov2 = jnp.pad(ov2, ((off, 0), (0, 0)))
    wspec = pl.BlockSpec((d, tn), lambda j, i: (0, j))
    cspec = pl.BlockSpec((tm, tn), lambda j, i: (0, j))
    return pl.pallas_call(
        functools.partial(_ffn_kernel, n_tiles=m // tm, sample_off=off, dec_seq=dec_seq),
        grid=(d_ff // tn, m // tm),
        in_specs=[pl.BlockSpec((tm, d), lambda j, i: (i, 0)),
                  pl.BlockSpec((8, d), lambda j, i: (jnp.maximum(i * (tm // 8) - 1, 0), 0)),
                  wspec, wspec, pl.BlockSpec((8, tn), lambda j, i: (0, j)), cspec, cspec],
        out_specs=pl.BlockSpec((tm, tn), lambda j, i: (i, j)),
        out_shape=jax.ShapeDtypeStruct((m, d_ff), BF16),
        compiler_params=pltpu.CompilerParams(
            dimension_semantics=("parallel", "arbitrary"), vmem_limit_bytes=VMEM_LIMIT),
        name="conv_ffn")(xf, xf, w_up, w_gate, cw, ov1, ov2)


def _rms_norm(x, g):
    return x * lax.rsqrt(jnp.mean(x * x, axis=-1, keepdims=True) + RMS_EPS) * g


def _pair_states(s):
    bsz, n_heads = s.shape[:2]
    st = jnp.swapaxes(s, -1, -2).reshape(bsz, n_heads // 2, 2, HEAD_DIM, HEAD_DIM)
    z = jnp.zeros_like(st[:, :, 0])
    top = jnp.concatenate([st[:, :, 0], z], axis=-1)
    bot = jnp.concatenate([z, st[:, :, 1]], axis=-1)
    return jnp.concatenate([top, bot], axis=-2)


def _unpair_states(sp):
    bsz, n_pairs = sp.shape[:2]
    h0 = sp[:, :, :HEAD_DIM, :HEAD_DIM]
    h1 = sp[:, :, HEAD_DIM:, HEAD_DIM:]
    st = jnp.stack([h0, h1], axis=2).reshape(bsz, 2 * n_pairs, HEAD_DIM, HEAD_DIM)
    return jnp.swapaxes(st, -1, -2)


def kernel(x_prompt, x_sample, cache_sb_k, cache_sb_v, state_shift, state_wkv, state_conv, page_table, norm_mix, w_in, mu_rkv, mu_x, w0, w1, w2, a0, a1, a2, g1, g2, mu_vr, v0, v1, v2, k_k, k_a, r_k, ln_x_w, ln_x_b, w_a, sb_bias, w_b, w_o, norm_ffn, w_up, w_gate, conv_w, conv_b, w_down, norm_final):
    depth = w_in.shape[0]
    bp, seq, d_model = x_prompt.shape
    bd, dec_seq, _ = x_sample.shape
    assert bp == 1, "one prompt sequence is concatenated with the sample rows"
    n_pool = cache_sb_k.shape[1]
    n_heads_a = state_wkv.shape[2]
    d_a = n_heads_a * HEAD_DIM
    n_heads_b = cache_sb_k.shape[3]
    d_b = n_heads_b * HEAD_DIM
    d_ff = w_up.shape[2]
    n_p = bp * seq
    n_s = bd * dec_seq
    n_pages = page_table.shape[1]
    assert n_heads_b * dec_seq == LANES, "sample queries of all heads fill one 128-row tile"

    pool_k = jnp.transpose(cache_sb_k, (0, 1, 3, 4, 2)).reshape(depth * n_pool, d_b, PAGE_SIZE)
    pool_v = jnp.transpose(cache_sb_v, (0, 1, 3, 4, 2)).reshape(depth * n_pool, d_b, PAGE_SIZE)
    pt_flat = page_table.reshape(-1).astype(jnp.int32)

    x = jnp.concatenate([x_prompt.reshape(n_p, d_model), x_sample.reshape(n_s, d_model)], axis=0)
    tm = n_s
    last_rows = jnp.concatenate([jnp.array([n_p - 1]), n_p + dec_seq * jnp.arange(bd) + dec_seq - 1])
    rkv0 = None
    outs = []
    for l in range(depth):
        w_in_b = w_in[l].astype(BF16)
        xcat = norm_shift(x, norm_mix[l], jnp.repeat(state_shift[l], dec_seq, axis=0), n_p, dec_seq)
        xn_last = _rms_norm(x[last_rows], norm_mix[l])

        c0 = 3 * d_a
        w_rkv = w_in[l][:, :c0]
        rkv = matmul(xcat, jnp.concatenate([w_rkv * (1.0 - mu_rkv[l]), w_rkv * mu_rkv[l]], axis=0).astype(BF16))
        q_b = matmul(xcat, (w_in[l][:, c0:c0 + d_b] * (LOG2E * HEAD_DIM ** -0.5)).astype(BF16), (BF16,))
        k_s, k_b = matmul(xcat, w_in_b[:, c0 + d_b:c0 + 2 * d_b], (F32, BF16))
        v_s, v_b = matmul(xcat, w_in_b[:, c0 + 2 * d_b:c0 + 3 * d_b], (F32, BF16))
        p_gate = matmul(xcat, w_in_b[:, c0 + 3 * d_b:])

        firsts = [w1[l], a1[l], g1[l]]
        mus = [mu_x[l, 0], mu_x[l, 1], mu_x[l, 2]]
        seconds = [w2[l], a2[l], g2[l]]
        if l > 0:
            firsts.append(v1[l - 1])
            mus.append(mu_vr[l - 1])
            seconds.append(v2[l - 1])
        widths = [f.shape[1] for f in firsts]
        tot = sum(widths)
        pad = (-tot) % LANES
        top = jnp.concatenate([(1.0 - m)[:, None] * f for m, f in zip(mus, firsts)]
                              + [jnp.zeros((d_model, pad), F32)], axis=1)
        bot = jnp.concatenate([m[:, None] * f for m, f in zip(mus, firsts)]
                              + [jnp.zeros((d_model, pad), F32)], axis=1)
        w_l1 = jnp.concatenate([top, bot], axis=0).astype(BF16)
        offs = [0]
        for wd in widths:
            offs.append(offs[-1] + wd)
        h2 = matmul(xcat, w_l1, (BF16,), act=(offs[1], offs[2], offs[3]))
        w_l2 = jnp.zeros((tot + pad, len(seconds) * d_a), F32)
        for i, s2 in enumerate(seconds):
            w_l2 = w_l2.at[offs[i]:offs[i + 1], i * d_a:(i + 1) * d_a].set(s2)
        o2 = matmul(h2, w_l2.astype(BF16))

        zero_row = jnp.zeros((d_a,), F32)
        prep_params = jnp.stack([w0[l], a0[l], k_k[l], k_a[l], v0[l - 1] if l > 0 else zero_row,
                                 zero_row, zero_row, zero_row])
        prep = wkv_prep(rkv, o2, rkv0 if l > 0 else None, prep_params, d_a, tm)
        if l == 0:
            rkv0 = rkv
            (log_decay, k_mod, a_in, b_in), v_src, v_col = prep, rkv, 2
        else:
            (log_decay, k_mod, a_in, b_in, v_src), v_col = prep, 0
        scan_in = (rkv, log_decay, k_mod, v_src, a_in, b_in)
        cols = (0, 0, 0, v_col, 0, 0)
        s0_p = jnp.zeros((bp, n_heads_a // 2, LANES, LANES), F32)
        y_p, sf_p = wkv7(*scan_in, s0_p, row0=0, n_seq=bp, n_chunks=seq // WKV_CHUNK, c_len=WKV_CHUNK,
                         col_blocks=cols)
        y_s, sf_s = wkv7(*scan_in, _pair_states(state_wkv[l]), row0=n_p, n_seq=bd, n_chunks=1, c_len=dec_seq,
                         col_blocks=cols)
        post_params = jnp.stack([ln_x_w[l], ln_x_b[l], r_k[l].reshape(d_a)] + [zero_row] * 5)
        za = wkv_post(jnp.concatenate([y_p, y_s], axis=0), rkv, k_mod, v_src, v_col, o2, post_params, d_a, tm)

        bias2 = sb_bias[l].astype(F32) * LOG2E
        att_p = sb_prompt(q_b, k_b, v_b, bias2, seq_len=n_p)
        q_smp = q_b[n_p:].reshape(bd, dec_seq, n_heads_b, HEAD_DIM)
        head_eye = jnp.eye(n_heads_b, dtype=BF16)
        qbd = jnp.einsum('bthd,hg->bhtgd', q_smp, head_eye).reshape(bd, n_heads_b * dec_seq, d_b)
        bias_col = jnp.repeat(bias2, dec_seq)[:, None]
        pad_rows = ((0, 0), (0, PAGE_SIZE - dec_seq), (0, 0))
        k_new = jnp.pad(k_b[n_p:].reshape(bd, dec_seq, d_b), pad_rows)
        v_new = jnp.pad(v_b[n_p:].reshape(bd, dec_seq, d_b), pad_rows)
        att_s = sb_sample(qbd, bias_col, k_new, v_new, pool_k, pool_v, pt_flat + l * n_pool,
                          dec_seq=dec_seq, n_heads=n_heads_b)
        att = jnp.concatenate([att_p, att_s.reshape(n_s, d_b)], axis=0)

        merged = branch_merge(za, att, w_a[l].astype(BF16), w_b[l].astype(BF16), p_gate)
        x, xf = proj_residual(merged, w_o[l].astype(BF16), x, norm_ffn[l], emit_x=True, norm_dtype=BF16)

        w_gate_b = w_gate[l].astype(BF16)
        cs = state_conv[l]
        s_idx = (jnp.arange(n_s) % dec_seq)[:, None]
        ov1 = jnp.repeat(cs[:, 1], dec_seq, axis=0)
        ov2 = jnp.where(s_idx == 0, jnp.repeat(cs[:, 0], dec_seq, axis=0), ov1)
        cw = jnp.concatenate([conv_w[l], conv_b[l][None], jnp.zeros((4, d_ff), F32)], axis=0)
        hidden = conv_ffn_hidden(xf, w_up[l].astype(BF16), w_gate_b, cw, ov1, ov2, n_p, dec_seq)
        tail = jnp.concatenate([jnp.array([n_p - 2, n_p - 1]),
                                (last_rows[1:, None] + jnp.array([-1, 0])).reshape(-1)])
        g_tail = matmul(jnp.pad(xf[tail], ((0, LANES - tail.shape[0]), (0, 0))), w_gate_b)
        if l < depth - 1:
            x = proj_residual(hidden, w_down[l].astype(BF16), x, None, emit_x=True, norm_dtype=None)
        else:
            y = proj_residual(hidden, w_down[l].astype(BF16), x, norm_final, emit_x=False, norm_dtype=F32)

        outs.append(dict(
            k_p=k_s[:n_p].reshape(bp, seq, n_heads_b, HEAD_DIM),
            v_p=v_s[:n_p].reshape(bp, seq, n_heads_b, HEAD_DIM),
            shift_p=xn_last[:1],
            wkv_p=_unpair_states(sf_p),
            conv_p=g_tail[:2][None],
            k_s=k_s[n_p:].reshape(bd, dec_seq, n_heads_b, HEAD_DIM),
            v_s=v_s[n_p:].reshape(bd, dec_seq, n_heads_b, HEAD_DIM),
            shift_s=xn_last[1:],
            wkv_s=_unpair_states(sf_s),
            conv_s=g_tail[2:2 + 2 * bd].reshape(bd, 2, d_ff)))

    def stack(name):
        return jnp.stack([o[name] for o in outs])

    return (y[:n_p].reshape(bp, seq, d_model), y[n_p:].reshape(bd, dec_seq, d_model),
            stack('shift_p'), stack('wkv_p'), stack('conv_p'), stack('k_p'), stack('v_p'),
            stack('shift_s'), stack('wkv_s'), stack('conv_s'), stack('k_s'), stack('v_s'))
```

```python
import functools

import jax
import jax.numpy as jnp
from jax import lax
from jax.experimental import pallas as pl
from jax.experimental.pallas import tpu as pltpu

F32 = jnp.float32
BF16 = jnp.bfloat16

HEAD_DIM = 64
LANES = 128
PAGE_SIZE = 128
RMS_EPS = 1e-6
GN_EPS = 64e-5
LOG2E = 1.4426950408889634
WKV_CHUNK = 64
VMEM_LIMIT = 56 * 1024 * 1024

NN = (((1,), (0,)), ((), ()))
NT = (((1,), (1,)), ((), ()))
TN = (((0,), (0,)), ((), ()))


def _dot(a, b, dims=NN):
    return lax.dot_general(a, b, dims, preferred_element_type=F32)


def _split2(x):
    hi = x.astype(BF16)
    lo = (x - hi.astype(F32)).astype(BF16)
    return hi, lo


def _dot3(a, b, dims=NN):
    ah, al = _split2(a)
    bh, bl = _split2(b)
    return _dot(ah, bh, dims) + (_dot(ah, bl, dims) + _dot(al, bh, dims))


def _pick(dim, candidates):
    for c in candidates:
        if dim % c == 0:
            return c
    return dim


def _lora_act(h, bounds):
    col = lax.broadcasted_iota(jnp.int32, h.shape, 1)
    is_tanh = col < bounds[0]
    is_sig = jnp.logical_and(col >= bounds[1], col < bounds[2])
    return jnp.where(is_tanh, jnp.tanh(h), jnp.where(is_sig, jax.nn.sigmoid(h), h))


def _mm_kernel(x_ref, w_ref, *rest, nk, act):
    o_refs = rest[:-1] if nk > 1 else rest
    k = pl.program_id(2)

    def finish(acc):
        if act is not None:
            acc = _lora_act(acc, act)
        for o_ref in o_refs:
            o_ref[...] = acc.astype(o_ref.dtype)

    if nk == 1:
        finish(_dot(x_ref[...], w_ref[...]))
        return
    acc_ref = rest[-1]

    @pl.when(k == 0)
    def _():
        acc_ref[...] = jnp.zeros_like(acc_ref)

    acc_ref[...] += _dot(x_ref[...], w_ref[...])

    @pl.when(k == nk - 1)
    def _():
        finish(acc_ref[...])


def matmul(x, w, out_dtypes=(F32,), act=None):
    m = x.shape[0]
    kd, n = w.shape
    tm = _pick(m, (768, 512, 256, 128, 64, 32, 16, 8))
    tn = _pick(n, (1024, 512, 256, 128))
    tk = _pick(kd, (2048, 1024, 512, 256, 128))
    nk = kd // tk
    outs = pl.pallas_call(
        functools.partial(_mm_kernel, nk=nk, act=act),
        grid=(m // tm, n // tn, nk),
        in_specs=[pl.BlockSpec((tm, tk), lambda i, j, k: (i, k)),
                  pl.BlockSpec((tk, tn), lambda i, j, k: (k, j))],
        out_specs=[pl.BlockSpec((tm, tn), lambda i, j, k: (i, j))] * len(out_dtypes),
        out_shape=[jax.ShapeDtypeStruct((m, n), dt) for dt in out_dtypes],
        scratch_shapes=[pltpu.VMEM((tm, tn), F32)] if nk > 1 else [],
        compiler_params=pltpu.CompilerParams(
            dimension_semantics=("parallel", "parallel", "arbitrary"), vmem_limit_bytes=VMEM_LIMIT),
        name="mm")(x, w)
    return outs[0] if len(out_dtypes) == 1 else outs


def _bf(x):
    return x.astype(BF16)


def _wkv_chunk_terms(rs, lws, ks, vs, as_, bs, c_len):
    n = len(rs)
    c2 = 2 * c_len
    row = lax.broadcasted_iota(jnp.int32, (c_len, c_len), 0)
    col = lax.broadcasted_iota(jnp.int32, (c_len, c_len), 1)
    eye = (col == row).astype(F32)
    tri = (col <= row).astype(BF16)
    r2 = lax.broadcasted_iota(jnp.int32, (c2, c2), 0)
    j2 = lax.broadcasted_iota(jnp.int32, (c2, c2), 1) & (c_len - 1)
    t2 = r2 & (c_len - 1)
    keep = jnp.logical_or(j2 < t2, jnp.logical_and(r2 >= c_len, j2 == t2))
    lane = lax.broadcasted_iota(jnp.int32, (c_len, LANES), 1)
    m0 = lane < HEAD_DIM
    masks = (m0, jnp.logical_not(m0))
    rr = lax.broadcasted_iota(jnp.int32, (LANES, LANES), 0)
    cc = lax.broadcasted_iota(jnp.int32, (LANES, LANES), 1)
    same_head = (rr < HEAD_DIM) == (cc < HEAD_DIM)
    diag = rr == cc
    n_double = max(c_len.bit_length() - 2, 0)
    zeros = jnp.zeros((c_len, LANES), F32)

    ats, rts, qs, bks, wcs, vpads = [], [], [], [], [], []
    for p in range(n):
        lw = lws[p]
        l1 = _bf(lw)
        rem = lw - l1.astype(F32)
        l2 = _bf(rem)
        l3 = _bf(rem - l2.astype(F32))
        lam = _dot(tri, l1) + (_dot(tri, l2) + _dot(tri, l3))
        lam_c = lam[c_len - 1:c_len, :]
        e_neg = jnp.exp(-lam)
        e_end = jnp.exp(lam_c - lam)
        ats.append(as_[p] * jnp.exp(lam - lw))
        rts.append(rs[p] * jnp.exp(lam))
        qs.append(_bf(jnp.concatenate([bs[p] * e_neg, ks[p] * e_neg], axis=0)))
        bks.append(_bf(jnp.concatenate([bs[p] * e_end, ks[p] * e_end], axis=0)))
        wcs.append(jnp.exp(lam_c))
        vpads.append(_bf(jnp.concatenate([zeros, vs[p]], axis=0)))

    units = [(p, h) for p in range(n) for h in range(2)]
    ms = []
    for p, h in units:
        lhs = jnp.concatenate([jnp.where(masks[h], ats[p], 0.0), jnp.where(masks[h], rts[p], 0.0)], axis=0)
        ms.append(jnp.where(keep, _dot(_bf(lhs), qs[p], NT), 0.0))
    tops = [_bf(m[:c_len]) for m in ms]
    bots = [_bf(m[c_len:]) for m in ms]
    pws = [m[:c_len, :c_len] for m in ms]
    tinvs = [eye + x for x in pws]
    for _ in range(n_double):
        pws = [_dot(_bf(x), _bf(x)) for x in pws]
        tinvs = [t + _dot(_bf(t), _bf(x)) for t, x in zip(tinvs, pws)]
    tinvs = [_bf(t) for t in tinvs]
    at_b = [_bf(x) for x in ats]
    tas = [_dot(t, at_b[p]) for t, (p, h) in zip(tinvs, units)]
    lakv = [_dot(top, vpads[p]) for top, (p, h) in zip(tops, units)]
    u0s = [_dot(t, _bf(x)) for t, x in zip(tinvs, lakv)]
    ta_p = [_bf(jnp.where(m0, tas[2 * p], tas[2 * p + 1])) for p in range(n)]
    u0_p = [jnp.where(m0, u0s[2 * p], u0s[2 * p + 1]) for p in range(n)]
    uv_p = [_bf(jnp.concatenate([u0_p[p], vs[p]], axis=0)) for p in range(n)]
    rcs = [_dot(bot[:, :c_len], ta_p[p]) for bot, (p, h) in zip(bots, units)]
    y0s = [_dot(bot, uv_p[p]) for bot, (p, h) in zip(bots, units)]
    rc = [rts[p] + jnp.where(m0, rcs[2 * p], rcs[2 * p + 1]) for p in range(n)]
    y0 = [jnp.where(m0, y0s[2 * p], y0s[2 * p + 1]) for p in range(n)]
    pc = [jnp.where(same_head, _dot(bks[p][:c_len], ta_p[p], TN), 0.0)
          + jnp.where(diag, jnp.broadcast_to(wcs[p], (LANES, LANES)), 0.0) for p in range(n)]
    d0 = [jnp.where(same_head, _dot(bks[p], uv_p[p], TN), 0.0) for p in range(n)]
    return rc, y0, pc, d0


def _wkv_terms_kernel(r_ref, lw_ref, k_ref, v_ref, a_ref, b_ref, rc_ref, y0_ref, pc_ref, d0_ref,
                      *, c_len, pairs, cps):
    units = [(c, p, slice(c * c_len, (c + 1) * c_len), slice(p * LANES, (p + 1) * LANES))
             for c in range(cps) for p in range(pairs)]
    rc, y0, pc, d0 = _wkv_chunk_terms(*[[ref[rs, ls] for _, _, rs, ls in units]
                                        for ref in (r_ref, lw_ref, k_ref, v_ref, a_ref, b_ref)], c_len)
    for u, (c, p, rs, ls) in enumerate(units):
        rc_ref[rs, ls] = rc[u]
        y0_ref[rs, ls] = y0[u]
        pc_ref[c, p] = pc[u]
        d0_ref[c, p] = d0[u]


def _wkv_sweep_kernel(rc_ref, y0_ref, pc_ref, d0_ref, s0_ref, y_ref, sout_ref, st_ref, *, n_chunks, n_pairs):
    c = pl.program_id(1)

    @pl.when(c == 0)
    def _():
        st_ref[...] = s0_ref[0]

    for p in range(n_pairs):
        sl = slice(p * LANES, (p + 1) * LANES)
        st = st_ref[p]
        y_ref[:, sl] = _dot3(rc_ref[:, sl], st) + y0_ref[:, sl]
        st_ref[p] = _dot3(pc_ref[0, p], st) + d0_ref[0, p]

    @pl.when(c == n_chunks - 1)
    def _():
        sout_ref[0] = st_ref[...]


def wkv7(r, lw, k, v, a, b, s0, *, row0, n_seq, n_chunks, c_len, col_blocks=(0, 0, 0, 0, 0, 0)):
    width = lw.shape[1]
    n_pairs = width // LANES
    pairs = n_pairs
    tot_chunks = n_seq * n_chunks
    rows = tot_chunks * c_len
    cps = 2 if tot_chunks % 2 == 0 and row0 % (2 * c_len) == 0 else 1
    blk_rows = cps * c_len
    blk0 = row0 // blk_rows
    lw_cols = pairs * LANES
    in_specs = [pl.BlockSpec((blk_rows, lw_cols), functools.partial(lambda c, g, cb: (blk0 + c, cb), cb=cb))
                for cb in col_blocks]
    row_spec = pl.BlockSpec((blk_rows, lw_cols), lambda c, g: (c, g))
    mat_spec = pl.BlockSpec((cps, pairs, LANES, LANES), lambda c, g: (c, g, 0, 0))
    rc, y0, pc, d0 = pl.pallas_call(
        functools.partial(_wkv_terms_kernel, c_len=c_len, pairs=pairs, cps=cps),
        grid=(tot_chunks // cps, n_pairs // pairs),
        in_specs=in_specs,
        out_specs=[row_spec, row_spec, mat_spec, mat_spec],
        out_shape=[jax.ShapeDtypeStruct((rows, width), F32), jax.ShapeDtypeStruct((rows, width), F32),
                   jax.ShapeDtypeStruct((tot_chunks, n_pairs, LANES, LANES), F32),
                   jax.ShapeDtypeStruct((tot_chunks, n_pairs, LANES, LANES), F32)],
        compiler_params=pltpu.CompilerParams(
            dimension_semantics=("parallel", "parallel"), vmem_limit_bytes=VMEM_LIMIT),
        name="wkv_terms")(r, lw, k, v, a, b)

    rowp = pl.BlockSpec((c_len, width), lambda s, c: (s * n_chunks + c, 0))
    matp = pl.BlockSpec((1, n_pairs, LANES, LANES), lambda s, c: (s * n_chunks + c, 0, 0, 0))
    stp = pl.BlockSpec((1, n_pairs, LANES, LANES), lambda s, c: (s, 0, 0, 0))
    y, s_fin = pl.pallas_call(
        functools.partial(_wkv_sweep_kernel, n_chunks=n_chunks, n_pairs=n_pairs),
        grid=(n_seq, n_chunks),
        in_specs=[rowp, rowp, matp, matp, stp],
        out_specs=[rowp, stp],
        out_shape=[jax.ShapeDtypeStruct((rows, width), F32),
                   jax.ShapeDtypeStruct((n_seq, n_pairs, LANES, LANES), F32)],
        scratch_shapes=[pltpu.VMEM((n_pairs, LANES, LANES), F32)],
        compiler_params=pltpu.CompilerParams(
            dimension_semantics=("parallel", "arbitrary"), vmem_limit_bytes=VMEM_LIMIT),
        name="wkv_sweep")(rc, y0, pc, d0, s0)
    return y, s_fin


def _tri_ones():
    jr = lax.broadcasted_iota(jnp.int32, (LANES, 2 * LANES), 0)
    jc = lax.broadcasted_iota(jnp.int32, (LANES, 2 * LANES), 1)
    return jnp.logical_or(jc >= LANES, jr >= jc).astype(BF16)


def _sb_sweep(zl, valid, carry, tri_ones):
    n_sub = zl.shape[1] // LANES
    sls = [slice(s * LANES, (s + 1) * LANES) for s in range(n_sub)]
    sps = []
    for sl in sls:
        z_s = zl[:, sl]
        sp = jnp.maximum(z_s, 0.0) + jnp.log(1.0 + jnp.exp2(-jnp.abs(z_s))) * LOG2E
        if valid is not None:
            sp = jnp.where(valid[:, sl], sp, 0.0)
        sps.append(sp.astype(BF16))
    exts = [_dot(sp, tri_ones) for sp in sps]
    atts = [None] * n_sub
    for s in reversed(range(n_sub)):
        att = jnp.exp2(zl[:, sls[s]] - (exts[s][:, :LANES] + carry))
        if valid is not None:
            att = jnp.where(valid[:, sls[s]], att, 0.0)
        atts[s] = att.astype(BF16)
        carry = carry + exts[s][:, LANES:]
    return (jnp.concatenate(atts, axis=1) if n_sub > 1 else atts[0]), carry


def _sb_prompt_kernel(q_ref, bias_ref, k_ref, v_ref, o_ref, *, tq, kb, n_par):
    i = pl.program_id(1)
    lane = lax.broadcasted_iota(jnp.int32, (tq, LANES), 1)
    m0 = lane < HEAD_DIM
    tri_ones = _tri_ones()
    lanes = [slice(p * LANES, (p + 1) * LANES) for p in range(n_par)]
    qs, biases = [], []
    for p, ls in enumerate(lanes):
        q = q_ref[:, ls]
        zero = jnp.zeros_like(q)
        qs.append(jnp.concatenate([jnp.where(m0, q, zero), jnp.where(m0, zero, q)], axis=0))
        biases.append(bias_ref[p])

    def logits(p, j):
        return _dot(qs[p], k_ref[pl.ds(pl.multiple_of(j * kb, kb), kb), lanes[p]], NT) + biases[p]

    def weighted(p, att, j):
        return _dot(att, v_ref[pl.ds(pl.multiple_of(j * kb, kb), kb), lanes[p]])

    jd = (i * tq) // kb
    row = lax.broadcasted_iota(jnp.int32, (2 * tq, kb), 0)
    col = lax.broadcasted_iota(jnp.int32, (2 * tq, kb), 1)
    valid = (jd * kb + col) < i * tq + jnp.where(row >= tq, row - tq, row)
    j0 = jnp.maximum(jd - 1, 0)
    init = []
    for p in range(n_par):
        att, carry = _sb_sweep(logits(p, jd), valid, jnp.zeros((2 * tq, LANES), F32), tri_ones)
        init.append((jnp.zeros((2 * tq, LANES), F32), carry, logits(p, j0), att))

    def body(jj, states):
        j = jd - 1 - jj
        out = []
        for p, (acc, carry, zl, att_prev) in enumerate(states):
            acc = acc + weighted(p, att_prev, j + 1)
            zl_next = logits(p, jnp.maximum(j - 1, 0))
            att, carry = _sb_sweep(zl, None, carry, tri_ones)
            out.append((acc, carry, zl_next, att))
        return tuple(out)

    final = lax.fori_loop(0, jd, body, tuple(init))
    for p, (acc, _, _, att) in enumerate(final):
        acc = acc + weighted(p, att, 0)
        o_ref[:, lanes[p]] = jnp.where(m0, acc[:tq], acc[tq:]).astype(o_ref.dtype)


def sb_prompt(q, k, v, bias, *, seq_len, tq=128, kb=512):
    width = q.shape[1]
    n_pairs = width // LANES
    nq = seq_len // tq
    n_par = next(n for n in (4, 2, 1) if n_pairs % n == 0)
    assert kb % tq == 0 and seq_len % kb == 0
    bias_cols = jnp.repeat(bias.astype(F32).reshape(n_pairs, 2), tq, axis=1).reshape(n_pairs, 2 * tq, 1)
    wide = n_par * LANES
    return pl.pallas_call(
        functools.partial(_sb_prompt_kernel, tq=tq, kb=kb, n_par=n_par),
        grid=(n_pairs // n_par, nq),
        in_specs=[pl.BlockSpec((tq, wide), lambda p, i: (i, p)),
                  pl.BlockSpec((n_par, 2 * tq, 1), lambda p, i: (p, 0, 0)),
                  pl.BlockSpec((seq_len, wide), lambda p, i: (0, p)),
                  pl.BlockSpec((seq_len, wide), lambda p, i: (0, p))],
        out_specs=pl.BlockSpec((tq, wide), lambda p, i: (i, p)),
        out_shape=jax.ShapeDtypeStruct((seq_len, width), BF16),
        compiler_params=pltpu.CompilerParams(
            dimension_semantics=("parallel", "arbitrary"), vmem_limit_bytes=VMEM_LIMIT),
        name="sb_prompt")(q, bias_cols, k, v)


def _sb_sample_kernel(pt_ref, qbd_ref, bias_ref, kn_ref, vn_ref, *rest, n_steps, dec_seq, n_heads, n_pg):
    kp_refs, vp_refs = rest[:n_pg], rest[n_pg:2 * n_pg]
    o_ref, acc_ref, carry_ref = rest[2 * n_pg:]
    j = pl.program_id(1)
    rows = qbd_ref.shape[1]
    tri_ones = _tri_ones()
    qbd = qbd_ref[0]
    bias = bias_ref[...]

    @pl.when(j == 0)
    def _():
        row = lax.broadcasted_iota(jnp.int32, (rows, PAGE_SIZE), 0)
        col = lax.broadcasted_iota(jnp.int32, (rows, PAGE_SIZE), 1)
        zl = _dot(qbd, kn_ref[0], NT) + bias
        att, carry = _sb_sweep(zl, col < row % dec_seq, jnp.zeros((rows, LANES), F32), tri_ones)
        acc_ref[...] = _dot(att, vn_ref[0])
        carry_ref[...] = carry

    @pl.when(j > 0)
    def _():
        kt = jnp.concatenate([kp_refs[i][0].astype(BF16) for i in range(n_pg)], axis=1)
        vt = jnp.concatenate([vp_refs[i][0].astype(BF16) for i in range(n_pg)], axis=1)
        att, carry = _sb_sweep(_dot(qbd, kt) + bias, None, carry_ref[...], tri_ones)
        acc_ref[...] += _dot(att, vt, NT)
        carry_ref[...] = carry

    @pl.when(j == n_steps - 1)
    def _():
        lane = lax.broadcasted_iota(jnp.int32, (dec_seq, n_heads * HEAD_DIM), 1)
        out = jnp.zeros((dec_seq, n_heads * HEAD_DIM), F32)
        for h in range(n_heads):
            blk = acc_ref[h * dec_seq:(h + 1) * dec_seq, :]
            out = jnp.where(lane // HEAD_DIM == h, blk, out)
        o_ref[0] = out.astype(o_ref.dtype)


def sb_sample(qbd, bias_col, k_new, v_new, pool_kt, pool_vt, page_ids, *, dec_seq, n_heads, n_pg=8):
    bsz, rows, width = qbd.shape
    n_pages = page_ids.shape[0] // bsz
    assert n_pages % n_pg == 0
    n_steps = n_pages // n_pg + 1

    def pool_idx(i):
        return lambda b, j, pt: (pt[b * n_pages + n_pages - n_pg * jnp.maximum(j, 1) + i], 0, 0)

    page_specs = [pl.BlockSpec((1, width, PAGE_SIZE), pool_idx(i)) for i in range(n_pg)]
    grid_spec = pltpu.PrefetchScalarGridSpec(
        num_scalar_prefetch=1, grid=(bsz, n_steps),
        in_specs=[pl.BlockSpec((1, rows, width), lambda b, j, pt: (b, 0, 0)),
                  pl.BlockSpec((rows, 1), lambda b, j, pt: (0, 0)),
                  pl.BlockSpec((1, PAGE_SIZE, width), lambda b, j, pt: (b, 0, 0)),
                  pl.BlockSpec((1, PAGE_SIZE, width), lambda b, j, pt: (b, 0, 0))] + page_specs + page_specs,
        out_specs=pl.BlockSpec((1, dec_seq, width), lambda b, j, pt: (b, 0, 0)),
        scratch_shapes=[pltpu.VMEM((rows, width), F32), pltpu.VMEM((rows, PAGE_SIZE), F32)])
    return pl.pallas_call(
        functools.partial(_sb_sample_kernel, n_steps=n_steps, dec_seq=dec_seq, n_heads=n_heads, n_pg=n_pg),
        grid_spec=grid_spec,
        out_shape=jax.ShapeDtypeStruct((bsz, dec_seq, width), BF16),
        compiler_params=pltpu.CompilerParams(
            dimension_semantics=("parallel", "arbitrary"), vmem_limit_bytes=VMEM_LIMIT),
        name="sb_sample")(page_ids, qbd, bias_col, k_new, v_new, *([pool_kt] * n_pg), *([pool_vt] * n_pg))


def _row_rms(x, g):
    return x * lax.rsqrt(jnp.mean(x * x, axis=-1, keepdims=True) + RMS_EPS) * g


def _norm_shift_kernel(x_ref, xp_ref, g_ref, first_ref, o_ref, *, n_prompt_tiles, dec_seq):
    i = pl.program_id(0)
    g = g_ref[...]
    xn = _row_rms(x_ref[...], g)
    tm, d = xn.shape
    above = _row_rms(xp_ref[...], g)[7:8]
    is_sample = i >= n_prompt_tiles
    first = jnp.where(is_sample, first_ref[...],
                      jnp.broadcast_to(above, xn.shape) * jnp.where(i > 0, 1.0, 0.0))
    row = lax.broadcasted_iota(jnp.int32, (tm, 1), 0)
    starts = (row & jnp.where(is_sample, dec_seq - 1, -1)) == 0
    prev = jnp.where(starts, first, pltpu.roll(xn, 1, 0))
    o_ref[:, :d] = xn.astype(BF16)
    o_ref[:, d:] = prev.astype(BF16)


def norm_shift(x, gain, first_rows, n_prompt, dec_seq):
    m, d = x.shape
    tm = first_rows.shape[0]
    assert n_prompt % tm == 0 and m == n_prompt + tm and dec_seq & (dec_seq - 1) == 0
    return pl.pallas_call(
        functools.partial(_norm_shift_kernel, n_prompt_tiles=n_prompt // tm, dec_seq=dec_seq),
        grid=(m // tm,),
        in_specs=[pl.BlockSpec((tm, d), lambda i: (i, 0)),
                  pl.BlockSpec((8, d), lambda i: (jnp.maximum(i * (tm // 8) - 1, 0), 0)),
                  pl.BlockSpec((1, d), lambda i: (0, 0)),
                  pl.BlockSpec((tm, d), lambda i: (0, 0))],
        out_specs=pl.BlockSpec((tm, 2 * d), lambda i: (i, 0)),
        out_shape=jax.ShapeDtypeStruct((m, 2 * d), BF16),
        compiler_params=pltpu.CompilerParams(dimension_semantics=("parallel",), vmem_limit_bytes=VMEM_LIMIT),
        name="norm_shift")(x, x, gain.reshape(1, d), first_rows)


def _seg_ones():
    r = lax.broadcasted_iota(jnp.int32, (2 * LANES, LANES), 0) & (LANES - 1)
    c = lax.broadcasted_iota(jnp.int32, (2 * LANES, LANES), 1)
    return ((r < HEAD_DIM) == (c < HEAD_DIM)).astype(BF16)


def _head_sums(x, seg):
    parts = [_dot(jnp.concatenate(_split2(x[:, t:t + LANES]), axis=1), seg)
             for t in range(0, x.shape[1], LANES)]
    return jnp.concatenate(parts, axis=1)


def _wkv_prep_kernel(rkv_ref, o2_ref, *rest, d_a, has_v):
    if has_v:
        vf_ref, p_ref, lw_ref, k_ref, a_ref, b_ref, v_ref = rest
    else:
        p_ref, lw_ref, k_ref, a_ref, b_ref = rest
    p = p_ref[...]
    seg = _seg_ones()
    k = rkv_ref[:, d_a:2 * d_a]
    dw = p[0:1] + o2_ref[:, :d_a]
    w_log = -(jnp.maximum(-dw, 0.0) + jnp.log(1.0 + jnp.exp(-jnp.abs(dw)))) - 0.5
    lw_ref[...] = -jnp.exp(w_log)
    a_rate = jax.nn.sigmoid(p[1:2] + o2_ref[:, d_a:2 * d_a])
    kk = k * p[2:3]
    kk = kk * lax.rsqrt(_head_sums(kk * kk, seg) + 1e-12)
    k_ref[...] = k * (1.0 + (a_rate - 1.0) * p[3:4])
    a_ref[...] = -kk
    b_ref[...] = kk * a_rate
    if has_v:
        v = rkv_ref[:, 2 * d_a:]
        v_ref[...] = v + (vf_ref[...] - v) * jax.nn.sigmoid(p[4:5] + o2_ref[:, 3 * d_a:])


def wkv_prep(rkv, o2, v_first_src, params, d_a, tm):
    m = rkv.shape[0]
    has_v = v_first_src is not None
    row = pl.BlockSpec((tm, d_a), lambda i: (i, 0))
    in_specs = [pl.BlockSpec((tm, rkv.shape[1]), lambda i: (i, 0)),
                pl.BlockSpec((tm, o2.shape[1]), lambda i: (i, 0))]
    args = [rkv, o2]
    if has_v:
        in_specs.append(pl.BlockSpec((tm, d_a), lambda i: (i, 2)))
        args.append(v_first_src)
    in_specs.append(pl.BlockSpec(params.shape, lambda i: (0, 0)))
    n_out = 5 if has_v else 4
    return pl.pallas_call(
        functools.partial(_wkv_prep_kernel, d_a=d_a, has_v=has_v),
        grid=(m // tm,), in_specs=in_specs, out_specs=[row] * n_out,
        out_shape=[jax.ShapeDtypeStruct((m, d_a), F32)] * n_out,
        compiler_params=pltpu.CompilerParams(dimension_semantics=("parallel",), vmem_limit_bytes=VMEM_LIMIT),
        name="wkv_prep")(*args, params)


def _wkv_post_kernel(y_ref, r_ref, k_ref, v_ref, g_ref, p_ref, o_ref):
    p = p_ref[...]
    seg = _seg_ones()
    y = y_ref[...]
    inv = 1.0 / HEAD_DIM
    d = y - _head_sums(y, seg) * inv
    var = _head_sums(d * d, seg) * inv
    y_n = d * lax.rsqrt(var + GN_EPS) * p[0:1] + p[1:2]
    bonus = _head_sums(r_ref[...] * k_ref[...] * p[2:3], seg) * v_ref[...]
    o_ref[...] = ((y_n + bonus) * g_ref[...]).astype(BF16)


def wkv_post(y, rkv, k_mod, v_src, v_col, o2, params, d_a, tm):
    m = y.shape[0]

    def col(c):
        return pl.BlockSpec((tm, d_a), lambda i: (i, c))

    return pl.pallas_call(
        _wkv_post_kernel, grid=(m // tm,),
        in_specs=[col(0), col(0), col(0), col(v_col), col(2), pl.BlockSpec(params.shape, lambda i: (0, 0))],
        out_specs=col(0), out_shape=jax.ShapeDtypeStruct((m, d_a), BF16),
        compiler_params=pltpu.CompilerParams(dimension_semantics=("parallel",), vmem_limit_bytes=VMEM_LIMIT),
        name="wkv_post")(y, rkv, k_mod, v_src, o2, params)


def _merge_kernel(za_ref, att_ref, wa_ref, wb_ref, ga_ref, gb_ref, o_ref):
    a = _dot(za_ref[...], wa_ref[...])
    b = _dot(att_ref[...], wb_ref[...])
    o_ref[...] = (jax.nn.sigmoid(ga_ref[...]) * a + jax.nn.sigmoid(gb_ref[...]) * b).astype(BF16)


def branch_merge(za, att, w_a, w_b, p_gate):
    m, kd = za.shape
    n = w_a.shape[1]
    tm = _pick(m, (768, 512, 256, 128))
    tn = _pick(n, (1024, 512, 256, 128))
    nj = n // tn
    return pl.pallas_call(
        _merge_kernel, grid=(m // tm, nj),
        in_specs=[pl.BlockSpec((tm, kd), lambda i, j: (i, 0)), pl.BlockSpec((tm, kd), lambda i, j: (i, 0)),
                  pl.BlockSpec((kd, tn), lambda i, j: (0, j)), pl.BlockSpec((kd, tn), lambda i, j: (0, j)),
                  pl.BlockSpec((tm, tn), lambda i, j: (i, j)), pl.BlockSpec((tm, tn), lambda i, j: (i, nj + j))],
        out_specs=pl.BlockSpec((tm, tn), lambda i, j: (i, j)),
        out_shape=jax.ShapeDtypeStruct((m, n), BF16),
        compiler_params=pltpu.CompilerParams(
            dimension_semantics=("parallel", "parallel"), vmem_limit_bytes=VMEM_LIMIT),
        name="branch_merge")(za, att, w_a, w_b, p_gate, p_gate)


def _proj_res_kernel(h_ref, w_ref, x_ref, g_ref, *rest, nk, emit_x, norm_dtype):
    acc_ref = rest[-1]
    k = pl.program_id(1)

    @pl.when(k == 0)
    def _():
        acc_ref[...] = jnp.zeros_like(acc_ref)

    acc_ref[...] += _dot(h_ref[...], w_ref[...])

    @pl.when(k == nk - 1)
    def _():
        x_new = x_ref[...] + acc_ref[...]
        outs = list(rest[:-1])
        if emit_x:
            outs.pop(0)[...] = x_new
        if norm_dtype is not None:
            outs.pop(0)[...] = _row_rms(x_new, g_ref[...]).astype(norm_dtype)


def proj_residual(h, w, x, gain, *, emit_x, norm_dtype):
    m, kd = h.shape
    d = w.shape[1]
    tm = _pick(m, (384, 256, 128))
    tk = _pick(kd, (2048, 1024, 512))
    nk = kd // tk
    row = pl.BlockSpec((tm, d), lambda i, k: (i, 0))
    out_shape, out_specs = [], []
    if emit_x:
        out_shape.append(jax.ShapeDtypeStruct((m, d), F32))
        out_specs.append(row)
    if norm_dtype is not None:
        out_shape.append(jax.ShapeDtypeStruct((m, d), norm_dtype))
        out_specs.append(row)
    gain = jnp.ones((d,), F32) if gain is None else gain
    outs = pl.pallas_call(
        functools.partial(_proj_res_kernel, nk=nk, emit_x=emit_x, norm_dtype=norm_dtype),
        grid=(m // tm, nk),
        in_specs=[pl.BlockSpec((tm, tk), lambda i, k: (i, k)), pl.BlockSpec((tk, d), lambda i, k: (k, 0)),
                  row, pl.BlockSpec((1, d), lambda i, k: (0, 0))],
        out_specs=out_specs, out_shape=out_shape,
        scratch_shapes=[pltpu.VMEM((tm, d), F32)],
        compiler_params=pltpu.CompilerParams(
            dimension_semantics=("parallel", "arbitrary"), vmem_limit_bytes=VMEM_LIMIT),
        name="proj_residual")(h, w, x, gain.reshape(1, d))
    return outs[0] if len(outs) == 1 else outs


def _ffn_kernel(xf_ref, xp_ref, wu_ref, wg_ref, cw_ref, ov1_ref, ov2_ref, o_ref, *, n_tiles, sample_off, dec_seq):
    i = pl.program_id(1)
    xf = xf_ref[...]
    xp = xp_ref[...]
    tm, tn = o_ref.shape
    row = lax.broadcasted_iota(jnp.int32, (tm, 1), 0)
    first_sample = jnp.where(i == n_tiles - 1, sample_off, tm)
    s = (row & (dec_seq - 1)) + jnp.where(row >= first_sample, 0, dec_seq)
    has_above = jnp.where(i > 0, 1.0, 0.0)
    sub = 2 * LANES
    for c0 in range(0, tn, sub):
        cs = slice(c0, c0 + sub)
        wg = wg_ref[:, cs]
        u = _dot(xf, wu_ref[:, cs])
        g = _dot(xf, wg)
        gp = _dot(xp, wg) * has_above
        g1 = jnp.where(row == 0, gp[7:8], pltpu.roll(g, 1, 0))
        g2 = jnp.where(row == 0, gp[6:7], jnp.where(row == 1, gp[7:8], pltpu.roll(g, 2, 0)))
        g1 = jnp.where(s == 0, ov1_ref[:, cs], g1)
        g2 = jnp.where(s < 2, ov2_ref[:, cs], g2)
        cw = cw_ref[:, cs]
        conv = cw[0:1] * g2 + cw[1:2] * g1 + cw[2:3] * g + cw[3:4]
        o_ref[:, cs] = (conv * jax.nn.sigmoid(conv) * u).astype(BF16)


def conv_ffn_hidden(xf, w_up, w_gate, cw, ov1, ov2, n_prompt, dec_seq):
    m, d = xf.shape
    d_ff = w_up.shape[1]
    n_s = ov1.shape[0]
    tm = next(t for t in (768, 512, 256, n_s) if m % t == 0 and (n_prompt % t) + n_s == t)
    off = n_prompt % tm
    tn = _pick(d_ff, (1024, 512, 256, 128))
    assert off % dec_seq == 0 and off % 8 == 0 and dec_seq >= 2
    ov1 = jnp.pad(ov1, ((off, 0), (0, 0)))
    ov2 = jnp.pad(ov2, ((off, 0), (0, 0)))
    wspec = pl.BlockSpec((d, tn), lambda j, i: (0, j))
    cspec = pl.BlockSpec((tm, tn), lambda j, i: (0, j))
    return pl.pallas_call(
        functools.partial(_ffn_kernel, n_tiles=m // tm, sample_off=off, dec_seq=dec_seq),
        grid=(d_ff // tn, m // tm),
        in_specs=[pl.BlockSpec((tm, d), lambda j, i: (i, 0)),
                  pl.BlockSpec((8, d), lambda j, i: (jnp.maximum(i * (tm // 8) - 1, 0), 0)),
                  wspec, wspec, pl.BlockSpec((8, tn), lambda j, i: (0, j)), cspec, cspec],
        out_specs=pl.BlockSpec((tm, tn), lambda j, i: (i, j)),
        out_shape=jax.ShapeDtypeStruct((m, d_ff), BF16),
        compiler_params=pltpu.CompilerParams(
            dimension_semantics=("parallel", "arbitrary"), vmem_limit_bytes=VMEM_LIMIT),
        name="conv_ffn")(xf, xf, w_up, w_gate, cw, ov1, ov2)


def _rms_norm(x, g):
    return x * lax.rsqrt(jnp.mean(x * x, axis=-1, keepdims=True) + RMS_EPS) * g


def _pair_states(s):
    bsz, n_heads = s.shape[:2]
    st = jnp.swapaxes(s, -1, -2).reshape(bsz, n_heads // 2, 2, HEAD_DIM, HEAD_DIM)
    z = jnp.zeros_like(st[:, :, 0])
    top = jnp.concatenate([st[:, :, 0], z], axis=-1)
    bot = jnp.concatenate([z, st[:, :, 1]], axis=-1)
    return jnp.concatenate([top, bot], axis=-2)


def _unpair_states(sp):
    bsz, n_pairs = sp.shape[:2]
    h0 = sp[:, :, :HEAD_DIM, :HEAD_DIM]
    h1 = sp[:, :, HEAD_DIM:, HEAD_DIM:]
    st = jnp.stack([h0, h1], axis=2).reshape(bsz, 2 * n_pairs, HEAD_DIM, HEAD_DIM)
    return jnp.swapaxes(st, -1, -2)


def kernel(x_prompt, x_sample, cache_sb_k, cache_sb_v, state_shift, state_wkv, state_conv, page_table, norm_mix, w_in, mu_rkv, mu_x, w0, w1, w2, a0, a1, a2, g1, g2, mu_vr, v0, v1, v2, k_k, k_a, r_k, ln_x_w, ln_x_b, w_a, sb_bias, w_b, w_o, norm_ffn, w_up, w_gate, conv_w, conv_b, w_down, norm_final):
    depth = w_in.shape[0]
    bp, seq, d_model = x_prompt.shape
    bd, dec_seq, _ = x_sample.shape
    assert bp == 1, "one prompt sequence is concatenated with the sample rows"
    n_pool = cache_sb_k.shape[1]
    n_heads_a = state_wkv.shape[2]
    d_a = n_heads_a * HEAD_DIM
    n_heads_b = cache_sb_k.shape[3]
    d_b = n_heads_b * HEAD_DIM
    d_ff = w_up.shape[2]
    n_p = bp * seq
    n_s = bd * dec_seq
    n_pages = page_table.shape[1]
    assert n_heads_b * dec_seq == LANES, "sample queries of all heads fill one 128-row tile"

    pool_k = jnp.transpose(cache_sb_k, (0, 1, 3, 4, 2)).reshape(depth * n_pool, d_b, PAGE_SIZE)
    pool_v = jnp.transpose(cache_sb_v, (0, 1, 3, 4, 2)).reshape(depth * n_pool, d_b, PAGE_SIZE)
    pt_flat = page_table.reshape(-1).astype(jnp.int32)

    x = jnp.concatenate([x_prompt.reshape(n_p, d_model), x_sample.reshape(n_s, d_model)], axis=0)
    tm = n_s
    last_rows = jnp.concatenate([jnp.array([n_p - 1]), n_p + dec_seq * jnp.arange(bd) + dec_seq - 1])
    rkv0 = None
    outs = []
    for l in range(depth):
        w_in_b = w_in[l].astype(BF16)
        xcat = norm_shift(x, norm_mix[l], jnp.repeat(state_shift[l], dec_seq, axis=0), n_p, dec_seq)
        xn_last = _rms_norm(x[last_rows], norm_mix[l])

        c0 = 3 * d_a
        w_rkv = w_in[l][:, :c0]
        rkv = matmul(xcat, jnp.concatenate([w_rkv * (1.0 - mu_rkv[l]), w_rkv * mu_rkv[l]], axis=0).astype(BF16))
        q_b = matmul(xcat, (w_in[l][:, c0:c0 + d_b] * (LOG2E * HEAD_DIM ** -0.5)).astype(BF16), (BF16,))
        k_s, k_b = matmul(xcat, w_in_b[:, c0 + d_b:c0 + 2 * d_b], (F32, BF16))
        v_s, v_b = matmul(xcat, w_in_b[:, c0 + 2 * d_b:c0 + 3 * d_b], (F32, BF16))
        p_gate = matmul(xcat, w_in_b[:, c0 + 3 * d_b:])

        firsts = [w1[l], a1[l], g1[l]]
        mus = [mu_x[l, 0], mu_x[l, 1], mu_x[l, 2]]
        seconds = [w2[l], a2[l], g2[l]]
        if l > 0:
            firsts.append(v1[l - 1])
            mus.append(mu_vr[l - 1])
            seconds.append(v2[l - 1])
        widths = [f.shape[1] for f in firsts]
        tot = sum(widths)
        pad = (-tot) % LANES
        top = jnp.concatenate([(1.0 - m)[:, None] * f for m, f in zip(mus, firsts)]
                              + [jnp.zeros((d_model, pad), F32)], axis=1)
        bot = jnp.concatenate([m[:, None] * f for m, f in zip(mus, firsts)]
                              + [jnp.zeros((d_model, pad), F32)], axis=1)
        w_l1 = jnp.concatenate([top, bot], axis=0).astype(BF16)
        offs = [0]
        for wd in widths:
            offs.append(offs[-1] + wd)
        h2 = matmul(xcat, w_l1, (BF16,), act=(offs[1], offs[2], offs[3]))
        w_l2 = jnp.zeros((tot + pad, len(seconds) * d_a), F32)
        for i, s2 in enumerate(seconds):
            w_l2 = w_l2.at[offs[i]:offs[i + 1], i * d_a:(i + 1) * d_a].set(s2)
        o2 = matmul(h2, w_l2.astype(BF16))

        zero_row = jnp.zeros((d_a,), F32)
        prep_params = jnp.stack([w0[l], a0[l], k_k[l], k_a[l], v0[l - 1] if l > 0 else zero_row,
                                 zero_row, zero_row, zero_row])
        prep = wkv_prep(rkv, o2, rkv0 if l > 0 else None, prep_params, d_a, tm)
        if l == 0:
            rkv0 = rkv
            (log_decay, k_mod, a_in, b_in), v_src, v_col = prep, rkv, 2
        else:
            (log_decay, k_mod, a_in, b_in, v_src), v_col = prep, 0
        scan_in = (rkv, log_decay, k_mod, v_src, a_in, b_in)
        cols = (0, 0, 0, v_col, 0, 0)
        s0_p = jnp.zeros((bp, n_heads_a // 2, LANES, LANES), F32)
        y_p, sf_p = wkv7(*scan_in, s0_p, row0=0, n_seq=bp, n_chunks=seq // WKV_CHUNK, c_len=WKV_CHUNK,
                         col_blocks=cols)
        y_s, sf_s = wkv7(*scan_in, _pair_states(state_wkv[l]), row0=n_p, n_seq=bd, n_chunks=1, c_len=dec_seq,
                         col_blocks=cols)
        post_params = jnp.stack([ln_x_w[l], ln_x_b[l], r_k[l].reshape(d_a)] + [zero_row] * 5)
        za = wkv_post(jnp.concatenate([y_p, y_s], axis=0), rkv, k_mod, v_src, v_col, o2, post_params, d_a, tm)

        bias2 = sb_bias[l].astype(F32) * LOG2E
        att_p = sb_prompt(q_b, k_b, v_b, bias2, seq_len=n_p)
        q_smp = q_b[n_p:].reshape(bd, dec_seq, n_heads_b, HEAD_DIM)
        head_eye = jnp.eye(n_heads_b, dtype=BF16)
        qbd = jnp.einsum('bthd,hg->bhtgd', q_smp, head_eye).reshape(bd, n_heads_b * dec_seq, d_b)
        bias_col = jnp.repeat(bias2, dec_seq)[:, None]
        pad_rows = ((0, 0), (0, PAGE_SIZE - dec_seq), (0, 0))
        k_new = jnp.pad(k_b[n_p:].reshape(bd, dec_seq, d_b), pad_rows)
        v_new = jnp.pad(v_b[n_p:].reshape(bd, dec_seq, d_b), pad_rows)
        att_s = sb_sample(qbd, bias_col, k_new, v_new, pool_k, pool_v, pt_flat + l * n_pool,
                          dec_seq=dec_seq, n_heads=n_heads_b)
        att = jnp.concatenate([att_p, att_s.reshape(n_s, d_b)], axis=0)

        merged = branch_merge(za, att, w_a[l].astype(BF16), w_b[l].astype(BF16), p_gate)
        x, xf = proj_residual(merged, w_o[l].astype(BF16), x, norm_ffn[l], emit_x=True, norm_dtype=BF16)

        w_gate_b = w_gate[l].astype(BF16)
        cs = state_conv[l]
        s_idx = (jnp.arange(n_s) % dec_seq)[:, None]
        ov1 = jnp.repeat(cs[:, 1], dec_seq, axis=0)
        ov2 = jnp.where(s_idx == 0, jnp.repeat(cs[:, 0], dec_seq, axis=0), ov1)
        cw = jnp.concatenate([conv_w[l], conv_b[l][None], jnp.zeros((4, d_ff), F32)], axis=0)
        hidden = conv_ffn_hidden(xf, w_up[l].astype(BF16), w_gate_b, cw, ov1, ov2, n_p, dec_seq)
        tail = jnp.concatenate([jnp.array([n_p - 2, n_p - 1]),
                                (last_rows[1:, None] + jnp.array([-1, 0])).reshape(-1)])
        g_tail = matmul(jnp.pad(xf[tail], ((0, LANES - tail.shape[0]), (0, 0))), w_gate_b)
        if l < depth - 1:
            x = proj_residual(hidden, w_down[l].astype(BF16), x, None, emit_x=True, norm_dtype=None)
        else:
            y = proj_residual(hidden, w_down[l].astype(BF16), x, norm_final, emit_x=False, norm_dtype=F32)

        outs.append(dict(
            k_p=k_s[:n_p].reshape(bp, seq, n_heads_b, HEAD_DIM),
            v_p=v_s[:n_p].reshape(bp, seq, n_heads_b, HEAD_DIM),
            shift_p=xn_last[:1],
            wkv_p=_unpair_states(sf_p),
            conv_p=g_tail[:2][None],
            k_s=k_s[n_p:].reshape(bd, dec_seq, n_heads_b, HEAD_DIM),
            v_s=v_s[n_p:].reshape(bd, dec_seq, n_heads_b, HEAD_DIM),
            shift_s=xn_last[1:],
            wkv_s=_unpair_states(sf_s),
            conv_s=g_tail[2:2 + 2 * bd].reshape(bd, 2, d_ff)))

    def stack(name):
        return jnp.stack([o[name] for o in outs])

    return (y[:n_p].reshape(bp, seq, d_model), y[n_p:].reshape(bd, dec_seq, d_model),
            stack('shift_p'), stack('wkv_p'), stack('conv_p'), stack('k_p'), stack('v_p'),
            stack('shift_s'), stack('wkv_s'), stack('conv_s'), stack('k_s'), stack('v_s'))
```

```python
import functools

import jax
import jax.numpy as jnp
from jax import lax
from jax.experimental import pallas as pl
from jax.experimental.pallas import tpu as pltpu

F32 = jnp.float32
BF16 = jnp.bfloat16

HEAD_DIM = 64
LANES = 128
PAGE_SIZE = 128
RMS_EPS = 1e-6
GN_EPS = 64e-5
LOG2E = 1.4426950408889634
WKV_CHUNK = 64
VMEM_LIMIT = 56 * 1024 * 1024

NN = (((1,), (0,)), ((), ()))
NT = (((1,), (1,)), ((), ()))
TN = (((0,), (0,)), ((), ()))


def _dot(a, b, dims=NN):
    return lax.dot_general(a, b, dims, preferred_element_type=F32)


def _split2(x):
    hi = x.astype(BF16)
    lo = (x - hi.astype(F32)).astype(BF16)
    return hi, lo


def _dot3(a, b, dims=NN):
    ah, al = _split2(a)
    bh, bl = _split2(b)
    return _dot(ah, bh, dims) + (_dot(ah, bl, dims) + _dot(al, bh, dims))


def _pick(dim, candidates):
    for c in candidates:
        if dim % c == 0:
            return c
    return dim


def _lora_act(h, bounds):
    col = lax.broadcasted_iota(jnp.int32, h.shape, 1)
    is_tanh = col < bounds[0]
    is_sig = jnp.logical_and(col >= bounds[1], col < bounds[2])
    return jnp.where(is_tanh, jnp.tanh(h), jnp.where(is_sig, jax.nn.sigmoid(h), h))


def _mm_kernel(x_ref, w_ref, *rest, nk, act):
    o_refs = rest[:-1] if nk > 1 else rest
    k = pl.program_id(2)

    def finish(acc):
        if act is not None:
            acc = _lora_act(acc, act)
        for o_ref in o_refs:
            o_ref[...] = acc.astype(o_ref.dtype)

    if nk == 1:
        finish(_dot(x_ref[...], w_ref[...]))
        return
    acc_ref = rest[-1]

    @pl.when(k == 0)
    def _():
        acc_ref[...] = jnp.zeros_like(acc_ref)

    acc_ref[...] += _dot(x_ref[...], w_ref[...])

    @pl.when(k == nk - 1)
    def _():
        finish(acc_ref[...])


def matmul(x, w, out_dtypes=(F32,), act=None):
    m = x.shape[0]
    kd, n = w.shape
    tm = _pick(m, (768, 512, 256, 128, 64, 32, 16, 8))
    tn = _pick(n, (1024, 512, 256, 128))
    tk = _pick(kd, (2048, 1024, 512, 256, 128))
    nk = kd // tk
    outs = pl.pallas_call(
        functools.partial(_mm_kernel, nk=nk, act=act),
        grid=(m // tm, n // tn, nk),
        in_specs=[pl.BlockSpec((tm, tk), lambda i, j, k: (i, k)),
                  pl.BlockSpec((tk, tn), lambda i, j, k: (k, j))],
        out_specs=[pl.BlockSpec((tm, tn), lambda i, j, k: (i, j))] * len(out_dtypes),
        out_shape=[jax.ShapeDtypeStruct((m, n), dt) for dt in out_dtypes],
        scratch_shapes=[pltpu.VMEM((tm, tn), F32)] if nk > 1 else [],
        compiler_params=pltpu.CompilerParams(
            dimension_semantics=("parallel", "parallel", "arbitrary"), vmem_limit_bytes=VMEM_LIMIT),
        name="mm")(x, w)
    return outs[0] if len(out_dtypes) == 1 else outs


def _bf(x):
    return x.astype(BF16)


def _wkv_chunk_terms(rs, lws, ks, vs, as_, bs, c_len):
    n = len(rs)
    c2 = 2 * c_len
    row = lax.broadcasted_iota(jnp.int32, (c_len, c_len), 0)
    col = lax.broadcasted_iota(jnp.int32, (c_len, c_len), 1)
    eye = (col == row).astype(F32)
    tri = (col <= row).astype(BF16)
    r2 = lax.broadcasted_iota(jnp.int32, (c2, c2), 0)
    j2 = lax.broadcasted_iota(jnp.int32, (c2, c2), 1) & (c_len - 1)
    t2 = r2 & (c_len - 1)
    keep = jnp.logical_or(j2 < t2, jnp.logical_and(r2 >= c_len, j2 == t2))
    lane = lax.broadcasted_iota(jnp.int32, (c_len, LANES), 1)
    m0 = lane < HEAD_DIM
    masks = (m0, jnp.logical_not(m0))
    rr = lax.broadcasted_iota(jnp.int32, (LANES, LANES), 0)
    cc = lax.broadcasted_iota(jnp.int32, (LANES, LANES), 1)
    same_head = (rr < HEAD_DIM) == (cc < HEAD_DIM)
    diag = rr == cc
    n_double = max(c_len.bit_length() - 2, 0)
    zeros = jnp.zeros((c_len, LANES), F32)

    ats, rts, qs, bks, wcs, vpads = [], [], [], [], [], []
    for p in range(n):
        lw = lws[p]
        l1 = _bf(lw)
        rem = lw - l1.astype(F32)
        l2 = _bf(rem)
        l3 = _bf(rem - l2.astype(F32))
        lam = _dot(tri, l1) + (_dot(tri, l2) + _dot(tri, l3))
        lam_c = lam[c_len - 1:c_len, :]
        e_neg = jnp.exp(-lam)
        e_end = jnp.exp(lam_c - lam)
        ats.append(as_[p] * jnp.exp(lam - lw))
        rts.append(rs[p] * jnp.exp(lam))
        qs.append(_bf(jnp.concatenate([bs[p] * e_neg, ks[p] * e_neg], axis=0)))
        bks.append(_bf(jnp.concatenate([bs[p] * e_end, ks[p] * e_end], axis=0)))
        wcs.append(jnp.exp(lam_c))
        vpads.append(_bf(jnp.concatenate([zeros, vs[p]], axis=0)))

    units = [(p, h) for p in range(n) for h in range(2)]
    ms = []
    for p, h in units:
        lhs = jnp.concatenate([jnp.where(masks[h], ats[p], 0.0), jnp.where(masks[h], rts[p], 0.0)], axis=0)
        ms.append(jnp.where(keep, _dot(_bf(lhs), qs[p], NT), 0.0))
    tops = [_bf(m[:c_len]) for m in ms]
    bots = [_bf(m[c_len:]) for m in ms]
    pws = [m[:c_len, :c_len] for m in ms]
    tinvs = [eye + x for x in pws]
    for _ in range(n_double):
        pws = [_dot(_bf(x), _bf(x)) for x in pws]
        tinvs = [t + _dot(_bf(t), _bf(x)) for t, x in zip(tinvs, pws)]
    tinvs = [_bf(t) for t in tinvs]
    at_b = [_bf(x) for x in ats]
    tas = [_dot(t, at_b[p]) for t, (p, h) in zip(tinvs, units)]
    lakv = [_dot(top, vpads[p]) for top, (p, h) in zip(tops, units)]
    u0s = [_dot(t, _bf(x)) for t, x in zip(tinvs, lakv)]
    ta_p = [_bf(jnp.where(m0, tas[2 * p], tas[2 * p + 1])) for p in range(n)]
    u0_p = [jnp.where(m0, u0s[2 * p], u0s[2 * p + 1]) for p in range(n)]
    uv_p = [_bf(jnp.concatenate([u0_p[p], vs[p]], axis=0)) for p in range(n)]
    rcs = [_dot(bot[:, :c_len], ta_p[p]) for bot, (p, h) in zip(bots, units)]
    y0s = [_dot(bot, uv_p[p]) for bot, (p, h) in zip(bots, units)]
    rc = [rts[p] + jnp.where(m0, rcs[2 * p], rcs[2 * p + 1]) for p in range(n)]
    y0 = [jnp.where(m0, y0s[2 * p], y0s[2 * p + 1]) for p in range(n)]
    pc = [jnp.where(same_head, _dot(bks[p][:c_len], ta_p[p], TN), 0.0)
          + jnp.where(diag, jnp.broadcast_to(wcs[p], (LANES, LANES)), 0.0) for p in range(n)]
    d0 = [jnp.where(same_head, _dot(bks[p], uv_p[p], TN), 0.0) for p in range(n)]
    return rc, y0, pc, d0


def _wkv_terms_kernel(r_ref, lw_ref, k_ref, v_ref, a_ref, b_ref, rc_ref, y0_ref, pc_ref, d0_ref,
                      *, c_len, pairs, cps):
    units = [(c, p, slice(c * c_len, (c + 1) * c_len), slice(p * LANES, (p + 1) * LANES))
             for c in range(cps) for p in range(pairs)]
    rc, y0, pc, d0 = _wkv_chunk_terms(*[[ref[rs, ls] for _, _, rs, ls in units]
                                        for ref in (r_ref, lw_ref, k_ref, v_ref, a_ref, b_ref)], c_len)
    for u, (c, p, rs, ls) in enumerate(units):
        rc_ref[rs, ls] = rc[u]
        y0_ref[rs, ls] = y0[u]
        pc_ref[c, p] = pc[u]
        d0_ref[c, p] = d0[u]


def _wkv_sweep_kernel(rc_ref, y0_ref, pc_ref, d0_ref, s0_ref, y_ref, sout_ref, st_ref, *, n_chunks, n_pairs):
    c = pl.program_id(1)

    @pl.when(c == 0)
    def _():
        st_ref[...] = s0_ref[0]

    for p in range(n_pairs):
        sl = slice(p * LANES, (p + 1) * LANES)
        st = st_ref[p]
        y_ref[:, sl] = _dot3(rc_ref[:, sl], st) + y0_ref[:, sl]
        st_ref[p] = _dot3(pc_ref[0, p], st) + d0_ref[0, p]

    @pl.when(c == n_chunks - 1)
    def _():
        sout_ref[0] = st_ref[...]


def wkv7(r, lw, k, v, a, b, s0, *, row0, n_seq, n_chunks, c_len, col_blocks=(0, 0, 0, 0, 0, 0)):
    width = lw.shape[1]
    n_pairs = width // LANES
    pairs = n_pairs
    tot_chunks = n_seq * n_chunks
    rows = tot_chunks * c_len
    cps = 2 if tot_chunks % 2 == 0 and row0 % (2 * c_len) == 0 else 1
    blk_rows = cps * c_len
    blk0 = row0 // blk_rows
    lw_cols = pairs * LANES
    in_specs = [pl.BlockSpec((blk_rows, lw_cols), functools.partial(lambda c, g, cb: (blk0 + c, cb), cb=cb))
                for cb in col_blocks]
    row_spec = pl.BlockSpec((blk_rows, lw_cols), lambda c, g: (c, g))
    mat_spec = pl.BlockSpec((cps, pairs, LANES, LANES), lambda c, g: (c, g, 0, 0))
    rc, y0, pc, d0 = pl.pallas_call(
        functools.partial(_wkv_terms_kernel, c_len=c_len, pairs=pairs, cps=cps),
        grid=(tot_chunks // cps, n_pairs // pairs),
        in_specs=in_specs,
        out_specs=[row_spec, row_spec, mat_spec, mat_spec],
        out_shape=[jax.ShapeDtypeStruct((rows, width), F32), jax.ShapeDtypeStruct((rows, width), F32),
                   jax.ShapeDtypeStruct((tot_chunks, n_pairs, LANES, LANES), F32),
                   jax.ShapeDtypeStruct((tot_chunks, n_pairs, LANES, LANES), F32)],
        compiler_params=pltpu.CompilerParams(
            dimension_semantics=("parallel", "parallel"), vmem_limit_bytes=VMEM_LIMIT),
        name="wkv_terms")(r, lw, k, v, a, b)

    rowp = pl.BlockSpec((c_len, width), lambda s, c: (s * n_chunks + c, 0))
    matp = pl.BlockSpec((1, n_pairs, LANES, LANES), lambda s, c: (s * n_chunks + c, 0, 0, 0))
    stp = pl.BlockSpec((1, n_pairs, LANES, LANES), lambda s, c: (s, 0, 0, 0))
    y, s_fin = pl.pallas_call(
        functools.partial(_wkv_sweep_kernel, n_chunks=n_chunks, n_pairs=n_pairs),
        grid=(n_seq, n_chunks),
        in_specs=[rowp, rowp, matp, matp, stp],
        out_specs=[rowp, stp],
        out_shape=[jax.ShapeDtypeStruct((rows, width), F32),
                   jax.ShapeDtypeStruct((n_seq, n_pairs, LANES, LANES), F32)],
        scratch_shapes=[pltpu.VMEM((n_pairs, LANES, LANES), F32)],
        compiler_params=pltpu.CompilerParams(
            dimension_semantics=("parallel", "arbitrary"), vmem_limit_bytes=VMEM_LIMIT),
        name="wkv_sweep")(rc, y0, pc, d0, s0)
    return y, s_fin


def _tri_ones():
    jr = lax.broadcasted_iota(jnp.int32, (LANES, 2 * LANES), 0)
    jc = lax.broadcasted_iota(jnp.int32, (LANES, 2 * LANES), 1)
    return jnp.logical_or(jc >= LANES, jr >= jc).astype(BF16)


def _sb_sweep(zl, valid, carry, tri_ones):
    n_sub = zl.shape[1] // LANES
    sls = [slice(s * LANES, (s + 1) * LANES) for s in range(n_sub)]
    sps = []
    for sl in sls:
        z_s = zl[:, sl]
        sp = jnp.maximum(z_s, 0.0) + jnp.log(1.0 + jnp.exp2(-jnp.abs(z_s))) * LOG2E
        if valid is not None:
            sp = jnp.where(valid[:, sl], sp, 0.0)
        sps.append(sp.astype(BF16))
    exts = [_dot(sp, tri_ones) for sp in sps]
    atts = [None] * n_sub
    for s in reversed(range(n_sub)):
        att = jnp.exp2(zl[:, sls[s]] - (exts[s][:, :LANES] + carry))
        if valid is not None:
            att = jnp.where(valid[:, sls[s]], att, 0.0)
        atts[s] = att.astype(BF16)
        carry = carry + exts[s][:, LANES:]
    return (jnp.concatenate(atts, axis=1) if n_sub > 1 else atts[0]), carry


def _sb_prompt_kernel(q_ref, bias_ref, k_ref, v_ref, o_ref, *, tq, kb, n_par):
    i = pl.program_id(1)
    lane = lax.broadcasted_iota(jnp.int32, (tq, LANES), 1)
    m0 = lane < HEAD_DIM
    tri_ones = _tri_ones()
    lanes = [slice(p * LANES, (p + 1) * LANES) for p in range(n_par)]
    qs, biases = [], []
    for p, ls in enumerate(lanes):
        q = q_ref[:, ls]
        zero = jnp.zeros_like(q)
        qs.append(jnp.concatenate([jnp.where(m0, q, zero), jnp.where(m0, zero, q)], axis=0))
        biases.append(bias_ref[p])

    def logits(p, j):
        return _dot(qs[p], k_ref[pl.ds(pl.multiple_of(j * kb, kb), kb), lanes[p]], NT) + biases[p]

    def weighted(p, att, j):
        return _dot(att, v_ref[pl.ds(pl.multiple_of(j * kb, kb), kb), lanes[p]])

    jd = (i * tq) // kb
    row = lax.broadcasted_iota(jnp.int32, (2 * tq, kb), 0)
    col = lax.broadcasted_iota(jnp.int32, (2 * tq, kb), 1)
    valid = (jd * kb + col) < i * tq + jnp.where(row >= tq, row - tq, row)
    j0 = jnp.maximum(jd - 1, 0)
    init = []
    for p in range(n_par):
        att, carry = _sb_sweep(logits(p, jd), valid, jnp.zeros((2 * tq, LANES), F32), tri_ones)
        init.append((jnp.zeros((2 * tq, LANES), F32), carry, logits(p, j0), att))

    def body(jj, states):
        j = jd - 1 - jj
        out = []
        for p, (acc, carry, zl, att_prev) in enumerate(states):
            acc = acc + weighted(p, att_prev, j + 1)
            zl_next = logits(p, jnp.maximum(j - 1, 0))
            att, carry = _sb_sweep(zl, None, carry, tri_ones)
            out.append((acc, carry, zl_next, att))
        return tuple(out)

    final = lax.fori_loop(0, jd, body, tuple(init))
    for p, (acc, _, _, att) in enumerate(final):
        acc = acc + weighted(p, att, 0)
        o_ref[:, lanes[p]] = jnp.where(m0, acc[:tq], acc[tq:]).astype(o_ref.dtype)


def sb_prompt(q, k, v, bias, *, seq_len, tq=128, kb=512):
    width = q.shape[1]
    n_pairs = width // LANES
    nq = seq_len // tq
    n_par = next(n for n in (4, 2, 1) if n_pairs % n == 0)
    assert kb % tq == 0 and seq_len % kb == 0
    bias_cols = jnp.repeat(bias.astype(F32).reshape(n_pairs, 2), tq, axis=1).reshape(n_pairs, 2 * tq, 1)
    wide = n_par * LANES
    return pl.pallas_call(
        functools.partial(_sb_prompt_kernel, tq=tq, kb=kb, n_par=n_par),
        grid=(n_pairs // n_par, nq),
        in_specs=[pl.BlockSpec((tq, wide), lambda p, i: (i, p)),
                  pl.BlockSpec((n_par, 2 * tq, 1), lambda p, i: (p, 0, 0)),
                  pl.BlockSpec((seq_len, wide), lambda p, i: (0, p)),
                  pl.BlockSpec((seq_len, wide), lambda p, i: (0, p))],
        out_specs=pl.BlockSpec((tq, wide), lambda p, i: (i, p)),
        out_shape=jax.ShapeDtypeStruct((seq_len, width), BF16),
        compiler_params=pltpu.CompilerParams(
            dimension_semantics=("parallel", "arbitrary"), vmem_limit_bytes=VMEM_LIMIT),
        name="sb_prompt")(q, bias_cols, k, v)


def _sb_sample_kernel(pt_ref, qbd_ref, bias_ref, kn_ref, vn_ref, *rest, n_steps, dec_seq, n_heads, n_pg):
    kp_refs, vp_refs = rest[:n_pg], rest[n_pg:2 * n_pg]
    o_ref, acc_ref, carry_ref = rest[2 * n_pg:]
    j = pl.program_id(1)
    rows = qbd_ref.shape[1]
    tri_ones = _tri_ones()
    qbd = qbd_ref[0]
    bias = bias_ref[...]

    @pl.when(j == 0)
    def _():
        row = lax.broadcasted_iota(jnp.int32, (rows, PAGE_SIZE), 0)
        col = lax.broadcasted_iota(jnp.int32, (rows, PAGE_SIZE), 1)
        zl = _dot(qbd, kn_ref[0], NT) + bias
        att, carry = _sb_sweep(zl, col < row % dec_seq, jnp.zeros((rows, LANES), F32), tri_ones)
        acc_ref[...] = _dot(att, vn_ref[0])
        carry_ref[...] = carry

    @pl.when(j > 0)
    def _():
        kt = jnp.concatenate([kp_refs[i][0].astype(BF16) for i in range(n_pg)], axis=1)
        vt = jnp.concatenate([vp_refs[i][0].astype(BF16) for i in range(n_pg)], axis=1)
        att, carry = _sb_sweep(_dot(qbd, kt) + bias, None, carry_ref[...], tri_ones)
        acc_ref[...] += _dot(att, vt, NT)
        carry_ref[...] = carry

    @pl.when(j == n_steps - 1)
    def _():
        lane = lax.broadcasted_iota(jnp.int32, (dec_seq, n_heads * HEAD_DIM), 1)
        out = jnp.zeros((dec_seq, n_heads * HEAD_DIM), F32)
        for h in range(n_heads):
            blk = acc_ref[h * dec_seq:(h + 1) * dec_seq, :]
            out = jnp.where(lane // HEAD_DIM == h, blk, out)
        o_ref[0] = out.astype(o_ref.dtype)


def sb_sample(qbd, bias_col, k_new, v_new, pool_kt, pool_vt, page_ids, *, dec_seq, n_heads, n_pg=8):
    bsz, rows, width = qbd.shape
    n_pages = page_ids.shape[0] // bsz
    assert n_pages % n_pg == 0
    n_steps = n_pages // n_pg + 1

    def pool_idx(i):
        return lambda b, j, pt: (pt[b * n_pages + n_pages - n_pg * jnp.maximum(j, 1) + i], 0, 0)

    page_specs = [pl.BlockSpec((1, width, PAGE_SIZE), pool_idx(i)) for i in range(n_pg)]
    grid_spec = pltpu.PrefetchScalarGridSpec(
        num_scalar_prefetch=1, grid=(bsz, n_steps),
        in_specs=[pl.BlockSpec((1, rows, width), lambda b, j, pt: (b, 0, 0)),
                  pl.BlockSpec((rows, 1), lambda b, j, pt: (0, 0)),
                  pl.BlockSpec((1, PAGE_SIZE, width), lambda b, j, pt: (b, 0, 0)),
                  pl.BlockSpec((1, PAGE_SIZE, width), lambda b, j, pt: (b, 0, 0))] + page_specs + page_specs,
        out_specs=pl.BlockSpec((1, dec_seq, width), lambda b, j, pt: (b, 0, 0)),
        scratch_shapes=[pltpu.VMEM((rows, width), F32), pltpu.VMEM((rows, PAGE_SIZE), F32)])
    return pl.pallas_call(
        functools.partial(_sb_sample_kernel, n_steps=n_steps, dec_seq=dec_seq, n_heads=n_heads, n_pg=n_pg),
        grid_spec=grid_spec,
        out_shape=jax.ShapeDtypeStruct((bsz, dec_seq, width), BF16),
        compiler_params=pltpu.CompilerParams(
            dimension_semantics=("parallel", "arbitrary"), vmem_limit_bytes=VMEM_LIMIT),
        name="sb_sample")(page_ids, qbd, bias_col, k_new, v_new, *([pool_kt] * n_pg), *([pool_vt] * n_pg))


def _row_rms(x, g):
    return x * lax.rsqrt(jnp.mean(x * x, axis=-1, keepdims=True) + RMS_EPS) * g


def _norm_shift_kernel(x_ref, xp_ref, g_ref, first_ref, o_ref, *, n_prompt_tiles, dec_seq):
    i = pl.program_id(0)
    g = g_ref[...]
    xn = _row_rms(x_ref[...], g)
    tm, d = xn.shape
    above = _row_rms(xp_ref[...], g)[7:8]
    is_sample = i >= n_prompt_tiles
    first = jnp.where(is_sample, first_ref[...],
                      jnp.broadcast_to(above, xn.shape) * jnp.where(i > 0, 1.0, 0.0))
    row = lax.broadcasted_iota(jnp.int32, (tm, 1), 0)
    starts = (row & jnp.where(is_sample, dec_seq - 1, -1)) == 0
    prev = jnp.where(starts, first, pltpu.roll(xn, 1, 0))
    o_ref[:, :d] = xn.astype(BF16)
    o_ref[:, d:] = prev.astype(BF16)


def norm_shift(x, gain, first_rows, n_prompt, dec_seq):
    m, d = x.shape
    tm = first_rows.shape[0]
    assert n_prompt % tm == 0 and m == n_prompt + tm and dec_seq & (dec_seq - 1) == 0
    return pl.pallas_call(
        functools.partial(_norm_shift_kernel, n_prompt_tiles=n_prompt // tm, dec_seq=dec_seq),
        grid=(m // tm,),
        in_specs=[pl.BlockSpec((tm, d), lambda i: (i, 0)),
                  pl.BlockSpec((8, d), lambda i: (jnp.maximum(i * (tm // 8) - 1, 0), 0)),
                  pl.BlockSpec((1, d), lambda i: (0, 0)),
                  pl.BlockSpec((tm, d), lambda i: (0, 0))],
        out_specs=pl.BlockSpec((tm, 2 * d), lambda i: (i, 0)),
        out_shape=jax.ShapeDtypeStruct((m, 2 * d), BF16),
        compiler_params=pltpu.CompilerParams(dimension_semantics=("parallel",), vmem_limit_bytes=VMEM_LIMIT),
        name="norm_shift")(x, x, gain.reshape(1, d), first_rows)


def _seg_ones():
    r = lax.broadcasted_iota(jnp.int32, (2 * LANES, LANES), 0) & (LANES - 1)
    c = lax.broadcasted_iota(jnp.int32, (2 * LANES, LANES), 1)
    return ((r < HEAD_DIM) == (c < HEAD_DIM)).astype(BF16)


def _head_sums(x, seg):
    parts = [_dot(jnp.concatenate(_split2(x[:, t:t + LANES]), axis=1), seg)
             for t in range(0, x.shape[1], LANES)]
    return jnp.concatenate(parts, axis=1)


def _wkv_prep_kernel(rkv_ref, o2_ref, *rest, d_a, has_v):
    if has_v:
        vf_ref, p_ref, lw_ref, k_ref, a_ref, b_ref, v_ref = rest
    else:
        p_ref, lw_ref, k_ref, a_ref, b_ref = rest
    p = p_ref[...]
    seg = _seg_ones()
    k = rkv_ref[:, d_a:2 * d_a]
    dw = p[0:1] + o2_ref[:, :d_a]
    w_log = -(jnp.maximum(-dw, 0.0) + jnp.log(1.0 + jnp.exp(-jnp.abs(dw)))) - 0.5
    lw_ref[...] = -jnp.exp(w_log)
    a_rate = jax.nn.sigmoid(p[1:2] + o2_ref[:, d_a:2 * d_a])
    kk = k * p[2:3]
    kk = kk * lax.rsqrt(_head_sums(kk * kk, seg) + 1e-12)
    k_ref[...] = k * (1.0 + (a_rate - 1.0) * p[3:4])
    a_ref[...] = -kk
    b_ref[...] = kk * a_rate
    if has_v:
        v = rkv_ref[:, 2 * d_a:]
        v_ref[...] = v + (vf_ref[...] - v) * jax.nn.sigmoid(p[4:5] + o2_ref[:, 3 * d_a:])


def wkv_prep(rkv, o2, v_first_src, params, d_a, tm):
    m = rkv.shape[0]
    has_v = v_first_src is not None
    row = pl.BlockSpec((tm, d_a), lambda i: (i, 0))
    in_specs = [pl.BlockSpec((tm, rkv.shape[1]), lambda i: (i, 0)),
                pl.BlockSpec((tm, o2.shape[1]), lambda i: (i, 0))]
    args = [rkv, o2]
    if has_v:
        in_specs.append(pl.BlockSpec((tm, d_a), lambda i: (i, 2)))
        args.append(v_first_src)
    in_specs.append(pl.BlockSpec(params.shape, lambda i: (0, 0)))
    n_out = 5 if has_v else 4
    return pl.pallas_call(
        functools.partial(_wkv_prep_kernel, d_a=d_a, has_v=has_v),
        grid=(m // tm,), in_specs=in_specs, out_specs=[row] * n_out,
        out_shape=[jax.ShapeDtypeStruct((m, d_a), F32)] * n_out,
        compiler_params=pltpu.CompilerParams(dimension_semantics=("parallel",), vmem_limit_bytes=VMEM_LIMIT),
        name="wkv_prep")(*args, params)


def _wkv_post_kernel(yp_ref, ys_ref, r_ref, k_ref, v_ref, g_ref, p_ref, o_ref, *, n_prompt_tiles):
    p = p_ref[...]
    seg = _seg_ones()
    y = jnp.where(pl.program_id(0) >= n_prompt_tiles, ys_ref[...], yp_ref[...])
    inv = 1.0 / HEAD_DIM
    d = y - _head_sums(y, seg) * inv
    var = _head_sums(d * d, seg) * inv
    y_n = d * lax.rsqrt(var + GN_EPS) * p[0:1] + p[1:2]
    bonus = _head_sums(r_ref[...] * k_ref[...] * p[2:3], seg) * v_ref[...]
    o_ref[...] = ((y_n + bonus) * g_ref[...]).astype(BF16)


def wkv_post(y_p, y_s, rkv, k_mod, v_src, v_col, o2, params, d_a, tm):
    n_pt = y_p.shape[0] // tm
    assert y_p.shape[0] % tm == 0 and y_s.shape[0] == tm
    m = y_p.shape[0] + tm

    def col(c):
        return pl.BlockSpec((tm, d_a), lambda i: (i, c))

    return pl.pallas_call(
        functools.partial(_wkv_post_kernel, n_prompt_tiles=n_pt), grid=(m // tm,),
        in_specs=[pl.BlockSpec((tm, d_a), lambda i: (jnp.minimum(i, n_pt - 1), 0)),
                  pl.BlockSpec((tm, d_a), lambda i: (0, 0)),
                  col(0), col(0), col(v_col), col(2), pl.BlockSpec(params.shape, lambda i: (0, 0))],
        out_specs=col(0), out_shape=jax.ShapeDtypeStruct((m, d_a), BF16),
        compiler_params=pltpu.CompilerParams(dimension_semantics=("parallel",), vmem_limit_bytes=VMEM_LIMIT),
        name="wkv_post")(y_p, y_s, rkv, k_mod, v_src, o2, params)


def _merge_kernel(za_ref, att_ref, wa_ref, wb_ref, ga_ref, gb_ref, o_ref):
    a = _dot(za_ref[...], wa_ref[...])
    b = _dot(att_ref[...], wb_ref[...])
    o_ref[...] = (jax.nn.sigmoid(ga_ref[...]) * a + jax.nn.sigmoid(gb_ref[...]) * b).astype(BF16)


def branch_merge(za, att, w_a, w_b, p_gate):
    m, kd = za.shape
    n = w_a.shape[1]
    tm = _pick(m, (768, 512, 256, 128))
    tn = _pick(n, (1024, 512, 256, 128))
    nj = n // tn
    return pl.pallas_call(
        _merge_kernel, grid=(m // tm, nj),
        in_specs=[pl.BlockSpec((tm, kd), lambda i, j: (i, 0)), pl.BlockSpec((tm, kd), lambda i, j: (i, 0)),
                  pl.BlockSpec((kd, tn), lambda i, j: (0, j)), pl.BlockSpec((kd, tn), lambda i, j: (0, j)),
                  pl.BlockSpec((tm, tn), lambda i, j: (i, j)), pl.BlockSpec((tm, tn), lambda i, j: (i, nj + j))],
        out_specs=pl.BlockSpec((tm, tn), lambda i, j: (i, j)),
        out_shape=jax.ShapeDtypeStruct((m, n), BF16),
        compiler_params=pltpu.CompilerParams(
            dimension_semantics=("parallel", "parallel"), vmem_limit_bytes=VMEM_LIMIT),
        name="branch_merge")(za, att, w_a, w_b, p_gate, p_gate)


def _proj_res_kernel(h_ref, w_ref, x_ref, g_ref, *rest, nk, emit_x, norm_dtype):
    acc_ref = rest[-1]
    k = pl.program_id(1)

    @pl.when(k == 0)
    def _():
        acc_ref[...] = jnp.zeros_like(acc_ref)

    acc_ref[...] += _dot(h_ref[...], w_ref[...])

    @pl.when(k == nk - 1)
    def _():
        x_new = x_ref[...] + acc_ref[...]
        outs = list(rest[:-1])
        if emit_x:
            outs.pop(0)[...] = x_new
        if norm_dtype is not None:
            outs.pop(0)[...] = _row_rms(x_new, g_ref[...]).astype(norm_dtype)


def proj_residual(h, w, x, gain, *, emit_x, norm_dtype):
    m, kd = h.shape
    d = w.shape[1]
    tm = _pick(m, (384, 256, 128))
    tk = _pick(kd, (2048, 1024, 512))
    nk = kd // tk
    row = pl.BlockSpec((tm, d), lambda i, k: (i, 0))
    out_shape, out_specs = [], []
    if emit_x:
        out_shape.append(jax.ShapeDtypeStruct((m, d), F32))
        out_specs.append(row)
    if norm_dtype is not None:
        out_shape.append(jax.ShapeDtypeStruct((m, d), norm_dtype))
        out_specs.append(row)
    gain = jnp.ones((d,), F32) if gain is None else gain
    outs = pl.pallas_call(
        functools.partial(_proj_res_kernel, nk=nk, emit_x=emit_x, norm_dtype=norm_dtype),
        grid=(m // tm, nk),
        in_specs=[pl.BlockSpec((tm, tk), lambda i, k: (i, k)), pl.BlockSpec((tk, d), lambda i, k: (k, 0)),
                  row, pl.BlockSpec((1, d), lambda i, k: (0, 0))],
        out_specs=out_specs, out_shape=out_shape,
        scratch_shapes=[pltpu.VMEM((tm, d), F32)],
        compiler_params=pltpu.CompilerParams(
            dimension_semantics=("parallel", "arbitrary"), vmem_limit_bytes=VMEM_LIMIT),
        name="proj_residual")(h, w, x, gain.reshape(1, d))
    return outs[0] if len(outs) == 1 else outs


def _ffn_kernel(xf_ref, xp_ref, wu_ref, wg_ref, cw_ref, ov1_ref, ov2_ref, o_ref, *, n_tiles, sample_off, dec_seq):
    i = pl.program_id(1)
    xf = xf_ref[...]
    xp = xp_ref[...]
    tm, tn = o_ref.shape
    row = lax.broadcasted_iota(jnp.int32, (tm, 1), 0)
    first_sample = jnp.where(i == n_tiles - 1, sample_off, tm)
    s = (row & (dec_seq - 1)) + jnp.where(row >= first_sample, 0, dec_seq)
    has_above = jnp.where(i > 0, 1.0, 0.0)
    sub = 2 * LANES
    for c0 in range(0, tn, sub):
        cs = slice(c0, c0 + sub)
        wg = wg_ref[:, cs]
        u = _dot(xf, wu_ref[:, cs])
        g = _dot(xf, wg)
        gp = _dot(xp, wg) * has_above
        g1 = jnp.where(row == 0, gp[7:8], pltpu.roll(g, 1, 0))
        g2 = jnp.where(row == 0, gp[6:7], jnp.where(row == 1, gp[7:8], pltpu.roll(g, 2, 0)))
        g1 = jnp.where(s == 0, ov1_ref[:, cs], g1)
        g2 = jnp.where(s < 2, ov2_ref[:, cs], g2)
        cw = cw_ref[:, cs]
        conv = cw[0:1] * g2 + cw[1:2] * g1 + cw[2:3] * g + cw[3:4]
        o_ref[:, cs] = (conv * jax.nn.sigmoid(conv) * u).astype(BF16)


def conv_ffn_hidden(xf, w_up, w_gate, cw, ov1, ov2, n_prompt, dec_seq):
    m, d = xf.shape
    d_ff = w_up.shape[1]
    n_s = ov1.shape[0]
    tm = next(t for t in (768, 512, 256, n_s) if m % t == 0 and (n_prompt % t) + n_s == t)
    off = n_prompt % tm
    tn = _pick(d_ff, (1024, 512, 256, 128))
    assert off % dec_seq == 0 and off % 8 == 0 and dec_seq >= 2
    ov1 = jnp.pad(ov1, ((off, 0), (0, 0)))
    ov2 = jnp.pad(ov2, ((off, 0), (0, 0)))
    wspec = pl.BlockSpec((d, tn), lambda j, i: (0, j))
    cspec = pl.BlockSpec((tm, tn), lambda j, i: (0, j))
    return pl.pallas_call(
        functools.partial(_ffn_kernel, n_tiles=m // tm, sample_off=off, dec_seq=dec_seq),
        grid=(d_ff // tn, m // tm),
        in_specs=[pl.BlockSpec((tm, d), lambda j, i: (i, 0)),
                  pl.BlockSpec((8, d), lambda j, i: (jnp.maximum(i * (tm // 8) - 1, 0), 0)),
                  wspec, wspec, pl.BlockSpec((8, tn), lambda j, i: (0, j)), cspec, cspec],
        out_specs=pl.BlockSpec((tm, tn), lambda j, i: (i, j)),
        out_shape=jax.ShapeDtypeStruct((m, d_ff), BF16),
        compiler_params=pltpu.CompilerParams(
            dimension_semantics=("parallel", "arbitrary"), vmem_limit_bytes=VMEM_LIMIT),
        name="conv_ffn")(xf, xf, w_up, w_gate, cw, ov1, ov2)


def _rms_norm(x, g):
    return x * lax.rsqrt(jnp.mean(x * x, axis=-1, keepdims=True) + RMS_EPS) * g


def _pair_states(s):
    bsz, n_heads = s.shape[:2]
    st = jnp.swapaxes(s, -1, -2).reshape(bsz, n_heads // 2, 2, HEAD_DIM, HEAD_DIM)
    z = jnp.zeros_like(st[:, :, 0])
    top = jnp.concatenate([st[:, :, 0], z], axis=-1)
    bot = jnp.concatenate([z, st[:, :, 1]], axis=-1)
    return jnp.concatenate([top, bot], axis=-2)


def _unpair_states(sp):
    bsz, n_pairs = sp.shape[:2]
    h0 = sp[:, :, :HEAD_DIM, :HEAD_DIM]
    h1 = sp[:, :, HEAD_DIM:, HEAD_DIM:]
    st = jnp.stack([h0, h1], axis=2).reshape(bsz, 2 * n_pairs, HEAD_DIM, HEAD_DIM)
    return jnp.swapaxes(st, -1, -2)


def kernel(x_prompt, x_sample, cache_sb_k, cache_sb_v, state_shift, state_wkv, state_conv, page_table, norm_mix, w_in, mu_rkv, mu_x, w0, w1, w2, a0, a1, a2, g1, g2, mu_vr, v0, v1, v2, k_k, k_a, r_k, ln_x_w, ln_x_b, w_a, sb_bias, w_b, w_o, norm_ffn, w_up, w_gate, conv_w, conv_b, w_down, norm_final):
    depth = w_in.shape[0]
    bp, seq, d_model = x_prompt.shape
    bd, dec_seq, _ = x_sample.shape
    assert bp == 1, "one prompt sequence is concatenated with the sample rows"
    n_pool = cache_sb_k.shape[1]
    n_heads_a = state_wkv.shape[2]
    d_a = n_heads_a * HEAD_DIM
    n_heads_b = cache_sb_k.shape[3]
    d_b = n_heads_b * HEAD_DIM
    d_ff = w_up.shape[2]
    n_p = bp * seq
    n_s = bd * dec_seq
    n_pages = page_table.shape[1]
    assert n_heads_b * dec_seq == LANES, "sample queries of all heads fill one 128-row tile"

    pool_k = jnp.transpose(cache_sb_k, (0, 1, 3, 4, 2)).reshape(depth * n_pool, d_b, PAGE_SIZE)
    pool_v = jnp.transpose(cache_sb_v, (0, 1, 3, 4, 2)).reshape(depth * n_pool, d_b, PAGE_SIZE)
    pt_flat = page_table.reshape(-1).astype(jnp.int32)

    x = jnp.concatenate([x_prompt.reshape(n_p, d_model), x_sample.reshape(n_s, d_model)], axis=0)
    tm = n_s
    last_rows = jnp.concatenate([jnp.array([n_p - 1]), n_p + dec_seq * jnp.arange(bd) + dec_seq - 1])
    rkv0 = None
    outs = []
    for l in range(depth):
        w_in_b = w_in[l].astype(BF16)
        xcat = norm_shift(x, norm_mix[l], jnp.repeat(state_shift[l], dec_seq, axis=0), n_p, dec_seq)
        xn_last = _rms_norm(x[last_rows], norm_mix[l])

        c0 = 3 * d_a
        w_rkv = w_in[l][:, :c0]
        rkv = matmul(xcat, jnp.concatenate([w_rkv * (1.0 - mu_rkv[l]), w_rkv * mu_rkv[l]], axis=0).astype(BF16))
        q_b = matmul(xcat, (w_in[l][:, c0:c0 + d_b] * (LOG2E * HEAD_DIM ** -0.5)).astype(BF16), (BF16,))
        k_s, k_b = matmul(xcat, w_in_b[:, c0 + d_b:c0 + 2 * d_b], (F32, BF16))
        v_s, v_b = matmul(xcat, w_in_b[:, c0 + 2 * d_b:c0 + 3 * d_b], (F32, BF16))
        p_gate = matmul(xcat, w_in_b[:, c0 + 3 * d_b:])

        firsts = [w1[l], a1[l], g1[l]]
        mus = [mu_x[l, 0], mu_x[l, 1], mu_x[l, 2]]
        seconds = [w2[l], a2[l], g2[l]]
        if l > 0:
            firsts.append(v1[l - 1])
            mus.append(mu_vr[l - 1])
            seconds.append(v2[l - 1])
        widths = [f.shape[1] for f in firsts]
        tot = sum(widths)
        pad = (-tot) % LANES
        top = jnp.concatenate([(1.0 - m)[:, None] * f for m, f in zip(mus, firsts)]
                              + [jnp.zeros((d_model, pad), F32)], axis=1)
        bot = jnp.concatenate([m[:, None] * f for m, f in zip(mus, firsts)]
                              + [jnp.zeros((d_model, pad), F32)], axis=1)
        w_l1 = jnp.concatenate([top, bot], axis=0).astype(BF16)
        offs = [0]
        for wd in widths:
            offs.append(offs[-1] + wd)
        h2 = matmul(xcat, w_l1, (BF16,), act=(offs[1], offs[2], offs[3]))
        w_l2 = jnp.zeros((tot + pad, len(seconds) * d_a), F32)
        for i, s2 in enumerate(seconds):
            w_l2 = w_l2.at[offs[i]:offs[i + 1], i * d_a:(i + 1) * d_a].set(s2)
        o2 = matmul(h2, w_l2.astype(BF16))

        zero_row = jnp.zeros((d_a,), F32)
        prep_params = jnp.stack([w0[l], a0[l], k_k[l], k_a[l], v0[l - 1] if l > 0 else zero_row,
                                 zero_row, zero_row, zero_row])
        prep = wkv_prep(rkv, o2, rkv0 if l > 0 else None, prep_params, d_a, tm)
        if l == 0:
            rkv0 = rkv
            (log_decay, k_mod, a_in, b_in), v_src, v_col = prep, rkv, 2
        else:
            (log_decay, k_mod, a_in, b_in, v_src), v_col = prep, 0
        scan_in = (rkv, log_decay, k_mod, v_src, a_in, b_in)
        cols = (0, 0, 0, v_col, 0, 0)
        s0_p = jnp.zeros((bp, n_heads_a // 2, LANES, LANES), F32)
        y_p, sf_p = wkv7(*scan_in, s0_p, row0=0, n_seq=bp, n_chunks=seq // WKV_CHUNK, c_len=WKV_CHUNK,
                         col_blocks=cols)
        y_s, sf_s = wkv7(*scan_in, _pair_states(state_wkv[l]), row0=n_p, n_seq=bd, n_chunks=1, c_len=dec_seq,
                         col_blocks=cols)
        post_params = jnp.stack([ln_x_w[l], ln_x_b[l], r_k[l].reshape(d_a)] + [zero_row] * 5)
        za = wkv_post(y_p, y_s, rkv, k_mod, v_src, v_col, o2, post_params, d_a, tm)

        bias2 = sb_bias[l].astype(F32) * LOG2E
        att_p = sb_prompt(q_b, k_b, v_b, bias2, seq_len=n_p)
        q_smp = q_b[n_p:].reshape(bd, dec_seq, n_heads_b, HEAD_DIM)
        head_eye = jnp.eye(n_heads_b, dtype=BF16)
        qbd = jnp.einsum('bthd,hg->bhtgd', q_smp, head_eye).reshape(bd, n_heads_b * dec_seq, d_b)
        bias_col = jnp.repeat(bias2, dec_seq)[:, None]
        pad_rows = ((0, 0), (0, PAGE_SIZE - dec_seq), (0, 0))
        k_new = jnp.pad(k_b[n_p:].reshape(bd, dec_seq, d_b), pad_rows)
        v_new = jnp.pad(v_b[n_p:].reshape(bd, dec_seq, d_b), pad_rows)
        att_s = sb_sample(qbd, bias_col, k_new, v_new, pool_k, pool_v, pt_flat + l * n_pool,
                          dec_seq=dec_seq, n_heads=n_heads_b)
        att = jnp.concatenate([att_p, att_s.reshape(n_s, d_b)], axis=0)

        merged = branch_merge(za, att, w_a[l].astype(BF16), w_b[l].astype(BF16), p_gate)
        x, xf = proj_residual(merged, w_o[l].astype(BF16), x, norm_ffn[l], emit_x=True, norm_dtype=BF16)

        w_gate_b = w_gate[l].astype(BF16)
        cs = state_conv[l]
        s_idx = (jnp.arange(n_s) % dec_seq)[:, None]
        ov1 = jnp.repeat(cs[:, 1], dec_seq, axis=0)
        ov2 = jnp.where(s_idx == 0, jnp.repeat(cs[:, 0], dec_seq, axis=0), ov1)
        cw = jnp.concatenate([conv_w[l], conv_b[l][None], jnp.zeros((4, d_ff), F32)], axis=0)
        hidden = conv_ffn_hidden(xf, w_up[l].astype(BF16), w_gate_b, cw, ov1, ov2, n_p, dec_seq)
        tail = jnp.concatenate([jnp.array([n_p - 2, n_p - 1]),
                                (last_rows[1:, None] + jnp.array([-1, 0])).reshape(-1)])
        g_tail = matmul(jnp.pad(xf[tail], ((0, LANES - tail.shape[0]), (0, 0))), w_gate_b)
        if l < depth - 1:
            x = proj_residual(hidden, w_down[l].astype(BF16), x, None, emit_x=True, norm_dtype=None)
        else:
            y = proj_residual(hidden, w_down[l].astype(BF16), x, norm_final, emit_x=False, norm_dtype=F32)

        outs.append(dict(
            k_p=k_s[:n_p].reshape(bp, seq, n_heads_b, HEAD_DIM),
            v_p=v_s[:n_p].reshape(bp, seq, n_heads_b, HEAD_DIM),
            shift_p=xn_last[:1],
            wkv_p=_unpair_states(sf_p),
            conv_p=g_tail[:2][None],
            k_s=k_s[n_p:].reshape(bd, dec_seq, n_heads_b, HEAD_DIM),
            v_s=v_s[n_p:].reshape(bd, dec_seq, n_heads_b, HEAD_DIM),
            shift_s=xn_last[1:],
            wkv_s=_unpair_states(sf_s),
            conv_s=g_tail[2:2 + 2 * bd].reshape(bd, 2, d_ff)))

    def stack(name):
        return jnp.stack([o[name] for o in outs])

    return (y[:n_p].reshape(bp, seq, d_model), y[n_p:].reshape(bd, dec_seq, d_model),
            stack('shift_p'), stack('wkv_p'), stack('conv_p'), stack('k_p'), stack('v_p'),
            stack('shift_s'), stack('wkv_s'), stack('conv_s'), stack('k_s'), stack('v_s'))
```
